```python
import math
import jax, jax.numpy as jnp
from jax import lax
import numpy as np

D_MODEL = 1024
BATCH = 4
SEQ = 4096
DEPTH = 2

CHUNK = 64
Q_BLOCK = 128
SB_HEADS = 16
SB_HEAD_DIM = D_MODEL // SB_HEADS
SB_WIDTH = SB_HEADS * SB_HEAD_DIM
CONV_WIDTH = D_MODEL
CONV_K = 3
N_BRANCH = 2
FFN_HIDDEN = -(-8 * D_MODEL // (3 * 256)) * 256
IN_WIDTH = 3 * SB_WIDTH + 3 * CONV_WIDTH + N_BRANCH * D_MODEL
EPS = 1e-6

kernel_name = "stickbreak_shortconv_griffin_adaln_block"


def rmsnorm(x, g):
    xf = x.astype(jnp.float32)
    y = xf * lax.rsqrt(jnp.mean(xf * xf, axis=-1, keepdims=True) + EPS)
    return (y * g.astype(jnp.float32)).astype(x.dtype)


def stick_breaking_attention(q, k, v):
    b, s_len, h, dh = q.shape
    qh = jnp.transpose(q, (0, 2, 1, 3)).astype(jnp.float32)
    kh = jnp.transpose(k, (0, 2, 1, 3)).astype(jnp.float32)
    vh = jnp.transpose(v, (0, 2, 1, 3)).astype(jnp.float32)
    inv_sqrt = 1.0 / math.sqrt(dh)
    outs = []
    for i in range(s_len // Q_BLOCK):
        t0 = i * Q_BLOCK
        n_keys = t0 + Q_BLOCK
        qb = qh[:, :, t0:t0 + Q_BLOCK]
        kp = kh[:, :, :n_keys]
        vp = vh[:, :, :n_keys]
        z = jnp.einsum('bhqd,bhkd->bhqk', qb, kp) * inv_sqrt
        t_idx = t0 + jnp.arange(Q_BLOCK)[:, None]
        s_idx = jnp.arange(n_keys)[None, :]
        mask = s_idx < t_idx
        log_not = jnp.where(mask, jax.nn.log_sigmoid(-z), 0.0)
        excl = lax.cumsum(log_not, axis=3, reverse=True) - log_not
        log_a = jax.nn.log_sigmoid(z) + excl
        a = jnp.where(mask, jnp.exp(log_a), 0.0)
        outs.append(jnp.einsum('bhqk,bhkd->bhqd', a, vp))
    o = jnp.concatenate(outs, axis=2)
    o = jnp.transpose(o, (0, 2, 1, 3)).reshape(b, s_len, h * dh)
    return o.astype(q.dtype)


def causal_dwconv(x, w):
    ch = x.shape[-1]
    return lax.conv_general_dilated(
        x, w[:, None, :].astype(x.dtype), window_strides=(1,),
        padding=[(CONV_K - 1, 0)], dimension_numbers=('NWC', 'WIO', 'NWC'),
        feature_group_count=ch)


def setup_inputs(seed: int = 0) -> dict:
    key = jax.random.key(seed)
    ks = jax.random.split(key, 18)
    f32 = jnp.float32

    def nrm(k, shape, fan_in):
        return jax.random.normal(k, shape, f32) * (fan_in ** -0.5)

    def gain(k, shape):
        return 1.0 + 0.02 * jax.random.normal(k, shape, f32)

    return {
        "x": jax.random.normal(ks[0], (BATCH, SEQ, D_MODEL), f32),
        "c": jax.random.normal(ks[1], (BATCH, D_MODEL), f32),
        "ada_w": nrm(ks[2], (DEPTH, D_MODEL, 6 * D_MODEL), D_MODEL),
        "ada_b": 0.02 * jax.random.normal(ks[3], (DEPTH, 6 * D_MODEL), f32),
        "ln1_g": gain(ks[4], (DEPTH, D_MODEL)),
        "w_in": nrm(ks[5], (DEPTH, D_MODEL, IN_WIDTH), D_MODEL),
        "q_norm_g": gain(ks[6], (DEPTH, SB_HEAD_DIM)),
        "k_norm_g": gain(ks[7], (DEPTH, SB_HEAD_DIM)),
        "conv_w": nrm(ks[8], (DEPTH, CONV_K, CONV_WIDTH), CONV_K),
        "w_branch_a": nrm(ks[9], (DEPTH, SB_WIDTH, D_MODEL), SB_WIDTH),
        "w_branch_b": nrm(ks[10], (DEPTH, CONV_WIDTH, D_MODEL), CONV_WIDTH),
        "w_out": nrm(ks[11], (DEPTH, D_MODEL, D_MODEL), D_MODEL),
        "ln2_g": gain(ks[12], (DEPTH, D_MODEL)),
        "w_ffn_gate": nrm(ks[13], (DEPTH, D_MODEL, FFN_HIDDEN), D_MODEL),
        "w_ffn_up": nrm(ks[14], (DEPTH, D_MODEL, FFN_HIDDEN), D_MODEL),
        "w_ffn_down": nrm(ks[15], (DEPTH, FFN_HIDDEN, D_MODEL), FFN_HIDDEN),
    }


def reference(x, c, ada_w, ada_b, ln1_g, w_in, q_norm_g, k_norm_g, conv_w,
              w_branch_a, w_branch_b, w_out, ln2_g, w_ffn_gate, w_ffn_up, w_ffn_down):
    b, s_len, d = x.shape
    split_at = np.cumsum([SB_WIDTH, SB_WIDTH, SB_WIDTH,
                          CONV_WIDTH, CONV_WIDTH, CONV_WIDTH, D_MODEL])
    c_act = jax.nn.silu(c)
    for l in range(DEPTH):
        mod = c_act @ ada_w[l] + ada_b[l]
        sh1, sc1, g1, sh2, sc2, g2 = [m[:, None, :] for m in jnp.split(mod, 6, axis=-1)]

        h = rmsnorm(x, ln1_g[l]) * (1.0 + sc1) + sh1
        p = h @ w_in[l]
        q, k, v, cb, cc, cx, ga, gb = jnp.split(p, split_at, axis=-1)
        q = rmsnorm(q.reshape(b, s_len, SB_HEADS, SB_HEAD_DIM), q_norm_g[l])
        k = rmsnorm(k.reshape(b, s_len, SB_HEADS, SB_HEAD_DIM), k_norm_g[l])
        v = v.reshape(b, s_len, SB_HEADS, SB_HEAD_DIM)
        y_a = stick_breaking_attention(q, k, v)
        y_b = cb * causal_dwconv(cc * cx, conv_w[l])
        merged = (jax.nn.sigmoid(ga) * (y_a @ w_branch_a[l])
                  + jax.nn.sigmoid(gb) * (y_b @ w_branch_b[l]))
        x = x + g1 * (merged @ w_out[l])

        h = rmsnorm(x, ln2_g[l]) * (1.0 + sc2) + sh2
        f = (jax.nn.silu(h @ w_ffn_gate[l]) * (h @ w_ffn_up[l])) @ w_ffn_down[l]
        x = x + g2 * f
    return x
```

```python
import functools
import math

import numpy as np
import jax
import jax.numpy as jnp
from jax import lax
from jax.experimental import pallas as pl
from jax.experimental.pallas import tpu as pltpu

D_MODEL = 1024
SB_HEADS = 16
SB_HEAD_DIM = 64
FFN_HIDDEN = 2816
IN_WIDTH = 8 * D_MODEL
CONV_K = 3
EPS = 1e-6

SUBLANES = 8
PERM_TILE = 256
SEG = PERM_TILE // SUBLANES
HEADS_PER_STEP = 2
ROW_TILE = 512
FFN_CHUNK = 1408
VMEM_LIMIT = 56 * 1024 * 1024

_BF16 = jnp.bfloat16
_F32 = jnp.float32


def _cparams(sem):
    return pltpu.CompilerParams(dimension_semantics=sem, vmem_limit_bytes=VMEM_LIMIT)


def _mod_kernel(c_ref, w_ref, b_ref, o_ref):
    c = c_ref[...]
    c_act = c * (1.0 / (1.0 + jnp.exp(-c)))
    o_ref[...] = (
        jnp.dot(c_act.astype(_BF16), w_ref[...].astype(_BF16), preferred_element_type=_F32)
        + b_ref[...]
    )


def _modulation(c_pad, ada_w, ada_b3):
    depth = ada_w.shape[0]
    n_chunks = ada_w.shape[2] // D_MODEL
    return pl.pallas_call(
        _mod_kernel,
        grid=(depth, n_chunks),
        in_specs=[
            pl.BlockSpec((SUBLANES, D_MODEL), lambda l, j: (0, 0)),
            pl.BlockSpec((None, D_MODEL, D_MODEL), lambda l, j: (l, 0, j)),
            pl.BlockSpec((None, 1, D_MODEL), lambda l, j: (l, 0, j)),
        ],
        out_specs=pl.BlockSpec((None, SUBLANES, D_MODEL), lambda l, j: (l, 0, j)),
        out_shape=jax.ShapeDtypeStruct((depth, SUBLANES, ada_w.shape[2]), _F32),
        compiler_params=_cparams(("arbitrary", "arbitrary")),
        name="adaln_mod",
    )(c_pad, ada_w, ada_b3)


def _rms_modulate(xf, g, sc, sh):
    ms = jnp.mean(xf * xf, axis=-1, keepdims=True)
    return (xf * lax.rsqrt(ms + EPS) * g) * (1.0 + sc) + sh


def _head_rms_scale(p, gmat):
    sq = (p * p).astype(_BF16)
    cols = []
    for c in range(D_MODEL // 256):
        cols.append(jnp.dot(sq[:, c * 256:(c + 1) * 256], gmat, preferred_element_type=_F32))
    ms = jnp.concatenate(cols, axis=1)
    return lax.rsqrt(ms + EPS)


def _permuted_causal_conv(u, cw_ref, prev_tail):
    w0 = cw_ref[0:1, :]
    w1 = cw_ref[1:2, :]
    w2 = cw_ref[2:3, :]
    sub = lax.broadcasted_iota(jnp.int32, (SUBLANES, u.shape[1]), 0)
    outs = []
    for a in range(u.shape[0] // PERM_TILE):
        t = u[a * PERM_TILE:(a + 1) * PERM_TILE]
        last = PERM_TILE - SUBLANES
        s31 = jnp.where(sub == 0, pltpu.roll(prev_tail[SUBLANES:], 1, 0), pltpu.roll(t[last:], 1, 0))
        s30 = jnp.where(sub == 0, pltpu.roll(prev_tail[:SUBLANES], 1, 0),
                        pltpu.roll(t[last - SUBLANES:last], 1, 0))
        prev1 = jnp.concatenate([s31, t[:last]], axis=0)
        prev2 = jnp.concatenate([s30, s31, t[:last - SUBLANES]], axis=0)
        outs.append(w2 * t + w1 * prev1 + w0 * prev2)
        prev_tail = t[last - SUBLANES:]
    return jnp.concatenate(outs, axis=0), prev_tail


def _inproj_kernel(x_ref, sc_ref, sh_ref, lng_ref, w_ref, gq_ref, gk_ref, gmat_ref, cw_ref,
                   q_ref, k_ref, vt_ref, yb_ref, sga_ref, sgb_ref,
                   h_scr, cb_scr, cc_scr, tail_scr, *, tiles_per_seq):
    m = pl.program_id(0)
    n = pl.program_id(1)

    @pl.when(n == 0)
    def _():
        h = _rms_modulate(x_ref[...], lng_ref[...], sc_ref[...], sh_ref[...])
        h_scr[...] = h.astype(_BF16)

    p = jnp.dot(h_scr[...], w_ref[...], preferred_element_type=_F32)

    @pl.when(n == 0)
    def _():
        q_ref[...] = (p * _head_rms_scale(p, gmat_ref[...]) * gq_ref[...]).astype(_BF16)

    @pl.when(n == 1)
    def _():
        k_ref[...] = (p * _head_rms_scale(p, gmat_ref[...]) * gk_ref[...]).astype(_BF16)

    @pl.when(n == 2)
    def _():
        vt_ref[...] = p.T.astype(_BF16)

    @pl.when(n == 3)
    def _():
        cb_scr[...] = p

    @pl.when(n == 4)
    def _():
        cc_scr[...] = p

    @pl.when(n == 5)
    def _():
        @pl.when(m % tiles_per_seq == 0)
        def _():
            tail_scr[...] = jnp.zeros_like(tail_scr)

        conv, tail = _permuted_causal_conv(cc_scr[...] * p, cw_ref, tail_scr[...])
        tail_scr[...] = tail
        yb_ref[...] = (cb_scr[...] * conv).astype(_BF16)

    @pl.when(n == 6)
    def _():
        sga_ref[...] = (1.0 / (1.0 + jnp.exp(-p))).astype(_BF16)

    @pl.when(n == 7)
    def _():
        sgb_ref[...] = (1.0 / (1.0 + jnp.exp(-p))).astype(_BF16)


def _inproj(x2, sc, sh, lng, w_bf, gq, gk, gmat, cw, batch, seq):
    rows = x2.shape[0]
    tiles_per_seq = seq // ROW_TILE
    n_blocks = IN_WIDTH // D_MODEL
    row_spec = pl.BlockSpec((ROW_TILE, D_MODEL), lambda m, n: (m, 0))
    mod_spec = pl.BlockSpec((None, 1, D_MODEL), lambda m, n: (m // tiles_per_seq, 0, 0))
    vec_spec = pl.BlockSpec((1, D_MODEL), lambda m, n: (0, 0))
    bf = jax.ShapeDtypeStruct((rows, D_MODEL), _BF16)
    return pl.pallas_call(
        functools.partial(_inproj_kernel, tiles_per_seq=tiles_per_seq),
        grid=(rows // ROW_TILE, n_blocks),
        in_specs=[
            row_spec, mod_spec, mod_spec, vec_spec,
            pl.BlockSpec((D_MODEL, D_MODEL), lambda m, n: (0, n)),
            vec_spec, vec_spec,
            pl.BlockSpec((256, 256), lambda m, n: (0, 0)),
            pl.BlockSpec((CONV_K, D_MODEL), lambda m, n: (0, 0)),
        ],
        out_specs=[
            row_spec, row_spec,
            pl.BlockSpec((None, D_MODEL, ROW_TILE),
                         lambda m, n: (m // tiles_per_seq, 0, m % tiles_per_seq)),
            row_spec, row_spec, row_spec,
        ],
        out_shape=[bf, bf, jax.ShapeDtypeStruct((batch, D_MODEL, seq), _BF16), bf, bf, bf],
        scratch_shapes=[
            pltpu.VMEM((ROW_TILE, D_MODEL), _BF16),
            pltpu.VMEM((ROW_TILE, D_MODEL), _F32),
            pltpu.VMEM((ROW_TILE, D_MODEL), _F32),
            pltpu.VMEM((2 * SUBLANES, D_MODEL), _F32),
        ],
        compiler_params=_cparams(("arbitrary", "arbitrary")),
        name="inproj",
    )(x2, sc, sh, lng, w_bf, gq, gk, gmat, cw)


def _suffix_products(tot):
    row = lax.broadcasted_iota(jnp.int32, tot.shape, 0)
    incl = tot
    for sh in (1, 2, 4):
        incl = incl * jnp.where(row < SUBLANES - sh, pltpu.roll(incl, SUBLANES - sh, 0), 1.0)
    excl = jnp.where(row < SUBLANES - 1, pltpu.roll(incl, SUBLANES - 1, 0), 1.0)
    total = jnp.broadcast_to(incl[0:1, :], tot.shape)
    return excl, total


def _attn_kernel(q_ref, k_ref, vt_ref, mask_ref, y_ref, acc_scr, carry_scr):
    qi = pl.program_id(2)
    qb = q_ref[...]
    lane = lax.broadcasted_iota(jnp.int32, qb.shape, 1)
    zero = jnp.zeros_like(qb)
    q_heads = (jnp.where(lane < SB_HEAD_DIM, qb, zero), jnp.where(lane >= SB_HEAD_DIM, qb, zero))

    acc_scr[...] = jnp.zeros_like(acc_scr)
    carry_scr[...] = jnp.ones_like(carry_scr)

    def key_tile(j, diag):
        start = pl.multiple_of(j * PERM_TILE, PERM_TILE)
        kt = k_ref[pl.ds(start, PERM_TILE), :]
        for h in range(HEADS_PER_STEP):
            z = lax.dot_general(kt, q_heads[h], (((1,), (1,)), ((), ())),
                                preferred_element_type=_F32)
            nb = 1.0 / (1.0 + jnp.exp2(z))
            if diag:
                nb = jnp.where(mask_ref[...] > 0.5, nb, 1.0)
            rows = [nb[i * SUBLANES:(i + 1) * SUBLANES, :] for i in range(SEG)]
            tot = rows[0]
            for i in range(1, SEG):
                tot = tot * rows[i]
            excl, total = _suffix_products(tot)
            carry = carry_scr[h]
            p = carry * excl
            carry_scr[h] = carry * total
            pieces = [None] * SEG
            for i in reversed(range(SEG)):
                pn = p * rows[i]
                pieces[i] = p - pn
                p = pn
            a_t = jnp.concatenate(pieces, axis=0).astype(_BF16)
            vt = vt_ref[h * SB_HEAD_DIM:(h + 1) * SB_HEAD_DIM, pl.ds(start, PERM_TILE)]
            acc_scr[h * SB_HEAD_DIM:(h + 1) * SB_HEAD_DIM, :] += jnp.dot(
                vt, a_t, preferred_element_type=_F32)

    key_tile(qi, True)

    def body(jj, c):
        key_tile(qi - 1 - jj, False)
        return c

    lax.fori_loop(0, qi, body, 0)
    y_ref[...] = acc_scr[...].T.astype(_BF16)


def _attention(q3, k3, vt3, mask):
    batch, seq, _ = q3.shape
    n_pairs = SB_HEADS // HEADS_PER_STEP
    lanes = HEADS_PER_STEP * SB_HEAD_DIM
    return pl.pallas_call(
        _attn_kernel,
        grid=(batch, n_pairs, seq // PERM_TILE),
        in_specs=[
            pl.BlockSpec((None, PERM_TILE, lanes), lambda b, hp, qi: (b, qi, hp)),
            pl.BlockSpec((None, seq, lanes), lambda b, hp, qi: (b, 0, hp)),
            pl.BlockSpec((None, lanes, seq), lambda b, hp, qi: (b, hp, 0)),
            pl.BlockSpec((PERM_TILE, PERM_TILE), lambda b, hp, qi: (0, 0)),
        ],
        out_specs=pl.BlockSpec((None, PERM_TILE, lanes), lambda b, hp, qi: (b, qi, hp)),
        out_shape=jax.ShapeDtypeStruct((batch, seq, D_MODEL), _BF16),
        scratch_shapes=[
            pltpu.VMEM((lanes, PERM_TILE), _F32),
            pltpu.VMEM((HEADS_PER_STEP, SUBLANES, PERM_TILE), _F32),
        ],
        compiler_params=_cparams(("arbitrary", "arbitrary", "arbitrary")),
        name="sb_attention",
    )(q3, k3, vt3, mask)


def _merge_kernel(ya_ref, yb_ref, sga_ref, sgb_ref, x_ref, wa_ref, wb_ref, wo_ref,
                  g1_ref, sc2_ref, sh2_ref, ln2_ref, x1_ref, h2_ref):
    ta = jnp.dot(ya_ref[...], wa_ref[...], preferred_element_type=_F32)
    tb = jnp.dot(yb_ref[...], wb_ref[...], preferred_element_type=_F32)
    merged = sga_ref[...].astype(_F32) * ta + sgb_ref[...].astype(_F32) * tb
    t = jnp.dot(merged.astype(_BF16), wo_ref[...], preferred_element_type=_F32)
    x1 = x_ref[...] + g1_ref[...] * t
    x1_ref[...] = x1
    h2_ref[...] = _rms_modulate(x1, ln2_ref[...], sc2_ref[...], sh2_ref[...]).astype(_BF16)


def _merge(ya, yb, sga, sgb, x2, wa, wb, wo, g1, sc2, sh2, ln2, seq):
    rows = x2.shape[0]
    tiles_per_seq = seq // ROW_TILE
    row_spec = pl.BlockSpec((ROW_TILE, D_MODEL), lambda m: (m, 0))
    w_spec = pl.BlockSpec((D_MODEL, D_MODEL), lambda m: (0, 0))
    mod_spec = pl.BlockSpec((None, 1, D_MODEL), lambda m: (m // tiles_per_seq, 0, 0))
    return pl.pallas_call(
        _merge_kernel,
        grid=(rows // ROW_TILE,),
        in_specs=[row_spec, row_spec, row_spec, row_spec, row_spec, w_spec, w_spec, w_spec,
                  mod_spec, mod_spec, mod_spec, pl.BlockSpec((1, D_MODEL), lambda m: (0, 0))],
        out_specs=[row_spec, row_spec],
        out_shape=[jax.ShapeDtypeStruct((rows, D_MODEL), _F32),
                   jax.ShapeDtypeStruct((rows, D_MODEL), _BF16)],
        compiler_params=_cparams(("arbitrary",)),
        name="merge_outproj",
    )(ya, yb, sga, sgb, x2, wa, wb, wo, g1, sc2, sh2, ln2)


def _ffn_kernel(h_ref, x_ref, wg_ref, wu_ref, wd_ref, g2_ref, o_ref):
    h = h_ref[...]
    acc = None
    for c in range(FFN_HIDDEN // FFN_CHUNK):
        cols = slice(c * FFN_CHUNK, (c + 1) * FFN_CHUNK)
        g = jnp.dot(h, wg_ref[:, cols], preferred_element_type=_F32)
        u = jnp.dot(h, wu_ref[:, cols], preferred_element_type=_F32)
        a = (g * (1.0 / (1.0 + jnp.exp(-g))) * u).astype(_BF16)
        d = jnp.dot(a, wd_ref[cols, :], preferred_element_type=_F32)
        acc = d if acc is None else acc + d
    o_ref[...] = x_ref[...] + g2_ref[...] * acc


def _ffn(h2, x1, wg, wu, wd, g2, seq):
    rows = x1.shape[0]
    tiles_per_seq = seq // ROW_TILE
    row_spec = pl.BlockSpec((ROW_TILE, D_MODEL), lambda m: (m, 0))
    return pl.pallas_call(
        _ffn_kernel,
        grid=(rows // ROW_TILE,),
        in_specs=[
            row_spec, row_spec,
            pl.BlockSpec((D_MODEL, FFN_HIDDEN), lambda m: (0, 0)),
            pl.BlockSpec((D_MODEL, FFN_HIDDEN), lambda m: (0, 0)),
            pl.BlockSpec((FFN_HIDDEN, D_MODEL), lambda m: (0, 0)),
            pl.BlockSpec((None, 1, D_MODEL), lambda m: (m // tiles_per_seq, 0, 0)),
        ],
        out_specs=row_spec,
        out_shape=jax.ShapeDtypeStruct((rows, D_MODEL), _F32),
        compiler_params=_cparams(("arbitrary",)),
        name="swiglu_ffn",
    )(h2, x1, wg, wu, wd, g2)


def _permute_tokens(x):
    b, s, d = x.shape
    return x.reshape(b, s // PERM_TILE, SUBLANES, SEG, d).transpose(0, 1, 3, 2, 4).reshape(b, s, d)


def _unpermute_tokens(x):
    b, s, d = x.shape
    return x.reshape(b, s // PERM_TILE, SEG, SUBLANES, d).transpose(0, 1, 3, 2, 4).reshape(b, s, d)


def _causal_mask():
    pos = np.arange(PERM_TILE)
    tok = (pos % SUBLANES) * SEG + pos // SUBLANES
    return jnp.asarray((tok[:, None] < tok[None, :]).astype(np.float32))


def _head_mean_matrix():
    idx = np.arange(256) // SB_HEAD_DIM
    return jnp.asarray((idx[:, None] == idx[None, :]).astype(np.float32) / SB_HEAD_DIM, dtype=_BF16)


def kernel(x, c, ada_w, ada_b, ln1_g, w_in, q_norm_g, k_norm_g, conv_w, w_branch_a, w_branch_b,
           w_out, ln2_g, w_ffn_gate, w_ffn_up, w_ffn_down):
    batch, seq, d = x.shape
    depth = ada_w.shape[0]
    assert d == D_MODEL and seq % ROW_TILE == 0 and ROW_TILE % PERM_TILE == 0
    assert batch <= SUBLANES and w_in.shape[2] == IN_WIDTH

    c_pad = jnp.zeros((SUBLANES, d), _F32).at[:batch].set(c)
    mod = _modulation(c_pad, ada_w, ada_b.reshape(depth, 1, -1))[:, :batch]
    mask = _causal_mask()
    gmat = _head_mean_matrix()
    q_scale = math.log2(math.e) / math.sqrt(SB_HEAD_DIM)

    xp = _permute_tokens(x).reshape(batch * seq, d)
    for l in range(depth):
        sh1, sc1, g1, sh2, sc2, g2 = [
            mod[l, :, i * d:(i + 1) * d].reshape(batch, 1, d) for i in range(6)]
        gq = jnp.tile(q_norm_g[l] * q_scale, SB_HEADS).reshape(1, d)
        gk = jnp.tile(k_norm_g[l], SB_HEADS).reshape(1, d)
        q, k, vt, yb, sga, sgb = _inproj(
            xp, sc1, sh1, ln1_g[l].reshape(1, d), w_in[l].astype(_BF16), gq, gk, gmat,
            conv_w[l], batch, seq)
        ya = _attention(q.reshape(batch, seq, d), k.reshape(batch, seq, d), vt, mask)
        x1, h2 = _merge(
            ya.reshape(batch * seq, d), yb, sga, sgb, xp,
            w_branch_a[l].astype(_BF16), w_branch_b[l].astype(_BF16), w_out[l].astype(_BF16),
            g1, sc2, sh2, ln2_g[l].reshape(1, d), seq)
        xp = _ffn(h2, x1, w_ffn_gate[l].astype(_BF16), w_ffn_up[l].astype(_BF16),
                  w_ffn_down[l].astype(_BF16), g2, seq)
    return _unpermute_tokens(xp.reshape(batch, seq, d))
```

```python
import functools
import math

import numpy as np
import jax
import jax.numpy as jnp
from jax import lax
from jax.experimental import pallas as pl
from jax.experimental.pallas import tpu as pltpu

D_MODEL = 1024
SB_HEADS = 16
SB_HEAD_DIM = 64
FFN_HIDDEN = 2816
IN_WIDTH = 8 * D_MODEL
CONV_K = 3
EPS = 1e-6

SUBLANES = 8
PERM_TILE = 256
SEG = PERM_TILE // SUBLANES
HEADS_PER_STEP = 2
SUB_CHAINS = 2
BLOCK_ORDER = (
    ("s", 0, 0), ("c", 0, 0), ("s", 0, 1), ("c", 0, 1), ("s", 1, 0), ("c", 0, 2), ("s", 1, 1),
    ("c", 0, 3), ("v", 0, 0), ("c", 1, 0), ("v", 0, 1), ("f", 0, 0), ("c", 1, 1), ("v", 1, 0),
    ("c", 1, 2), ("v", 1, 1), ("f", 1, 0), ("c", 1, 3))
ROW_TILE = 512
FFN_CHUNK = 1408
VMEM_LIMIT = 56 * 1024 * 1024

_BF16 = jnp.bfloat16
_F32 = jnp.float32


def _cparams(sem):
    return pltpu.CompilerParams(dimension_semantics=sem, vmem_limit_bytes=VMEM_LIMIT)


def _mod_kernel(c_ref, w_ref, b_ref, o_ref):
    c = c_ref[...]
    c_act = c * (1.0 / (1.0 + jnp.exp(-c)))
    o_ref[...] = (
        jnp.dot(c_act.astype(_BF16), w_ref[...].astype(_BF16), preferred_element_type=_F32)
        + b_ref[...]
    )


def _modulation(c_pad, ada_w, ada_b3):
    depth = ada_w.shape[0]
    n_chunks = ada_w.shape[2] // D_MODEL
    return pl.pallas_call(
        _mod_kernel,
        grid=(depth, n_chunks),
        in_specs=[
            pl.BlockSpec((SUBLANES, D_MODEL), lambda l, j: (0, 0)),
            pl.BlockSpec((None, D_MODEL, D_MODEL), lambda l, j: (l, 0, j)),
            pl.BlockSpec((None, 1, D_MODEL), lambda l, j: (l, 0, j)),
        ],
        out_specs=pl.BlockSpec((None, SUBLANES, D_MODEL), lambda l, j: (l, 0, j)),
        out_shape=jax.ShapeDtypeStruct((depth, SUBLANES, ada_w.shape[2]), _F32),
        compiler_params=_cparams(("arbitrary", "arbitrary")),
        name="adaln_mod",
    )(c_pad, ada_w, ada_b3)


def _rms_modulate(xf, g, sc, sh):
    ms = jnp.mean(xf * xf, axis=-1, keepdims=True)
    return (xf * lax.rsqrt(ms + EPS) * g) * (1.0 + sc) + sh


def _head_rms_scale(p, gmat):
    sq = (p * p).astype(_BF16)
    cols = []
    for c in range(D_MODEL // 256):
        cols.append(jnp.dot(sq[:, c * 256:(c + 1) * 256], gmat, preferred_element_type=_F32))
    ms = jnp.concatenate(cols, axis=1)
    return lax.rsqrt(ms + EPS)


def _permuted_causal_conv(u, cw_ref, prev_tail):
    w0 = cw_ref[0:1, :]
    w1 = cw_ref[1:2, :]
    w2 = cw_ref[2:3, :]
    sub = lax.broadcasted_iota(jnp.int32, (SUBLANES, u.shape[1]), 0)
    outs = []
    for a in range(u.shape[0] // PERM_TILE):
        t = u[a * PERM_TILE:(a + 1) * PERM_TILE]
        last = PERM_TILE - SUBLANES
        s31 = jnp.where(sub == 0, pltpu.roll(prev_tail[SUBLANES:], 1, 0), pltpu.roll(t[last:], 1, 0))
        s30 = jnp.where(sub == 0, pltpu.roll(prev_tail[:SUBLANES], 1, 0),
                        pltpu.roll(t[last - SUBLANES:last], 1, 0))
        prev1 = jnp.concatenate([s31, t[:last]], axis=0)
        prev2 = jnp.concatenate([s30, s31, t[:last - SUBLANES]], axis=0)
        outs.append(w2 * t + w1 * prev1 + w0 * prev2)
        prev_tail = t[last - SUBLANES:]
    return jnp.concatenate(outs, axis=0), prev_tail


def _inproj_kernel(x_ref, sc_ref, sh_ref, lng_ref, w_ref, gq_ref, gk_ref, gmat_ref, cw_ref,
                   q0_ref, q1_ref, k_ref, vt_ref, yb_ref, sga_ref, sgb_ref,
                   h_scr, cb_scr, cc_scr, tail_scr, *, tiles_per_seq):
    m = pl.program_id(0)
    n = pl.program_id(1)

    @pl.when(n == 0)
    def _():
        h = _rms_modulate(x_ref[...], lng_ref[...], sc_ref[...], sh_ref[...])
        h_scr[...] = h.astype(_BF16)

    p = jnp.dot(h_scr[...], w_ref[...], preferred_element_type=_F32)

    @pl.when(n == 0)
    def _():
        qn = p * _head_rms_scale(p, gmat_ref[...]) * gq_ref[...]
        lane = lax.broadcasted_iota(jnp.int32, qn.shape, 1)
        even = (lane % (2 * SB_HEAD_DIM)) < SB_HEAD_DIM
        q0_ref[...] = jnp.where(even, qn, 0.0).astype(_BF16)
        q1_ref[...] = jnp.where(even, 0.0, qn).astype(_BF16)

    @pl.when(n == 1)
    def _():
        k_ref[...] = (p * _head_rms_scale(p, gmat_ref[...]) * gk_ref[...]).astype(_BF16)

    @pl.when(n == 2)
    def _():
        vt_ref[...] = p.T.astype(_BF16)

    @pl.when(n == 3)
    def _():
        cb_scr[...] = p

    @pl.when(n == 4)
    def _():
        cc_scr[...] = p

    @pl.when(n == 5)
    def _():
        @pl.when(m % tiles_per_seq == 0)
        def _():
            tail_scr[...] = jnp.zeros_like(tail_scr)

        conv, tail = _permuted_causal_conv(cc_scr[...] * p, cw_ref, tail_scr[...])
        tail_scr[...] = tail
        yb_ref[...] = (cb_scr[...] * conv).astype(_BF16)

    @pl.when(n == 6)
    def _():
        sga_ref[...] = (1.0 / (1.0 + jnp.exp(-p))).astype(_BF16)

    @pl.when(n == 7)
    def _():
        sgb_ref[...] = (1.0 / (1.0 + jnp.exp(-p))).astype(_BF16)


def _inproj(x2, sc, sh, lng, w_bf, gq, gk, gmat, cw, batch, seq):
    rows = x2.shape[0]
    tiles_per_seq = seq // ROW_TILE
    n_blocks = IN_WIDTH // D_MODEL
    row_spec = pl.BlockSpec((ROW_TILE, D_MODEL), lambda m, n: (m, 0))
    mod_spec = pl.BlockSpec((None, 1, D_MODEL), lambda m, n: (m // tiles_per_seq, 0, 0))
    vec_spec = pl.BlockSpec((1, D_MODEL), lambda m, n: (0, 0))
    bf = jax.ShapeDtypeStruct((rows, D_MODEL), _BF16)
    return pl.pallas_call(
        functools.partial(_inproj_kernel, tiles_per_seq=tiles_per_seq),
        grid=(rows // ROW_TILE, n_blocks),
        in_specs=[
            row_spec, mod_spec, mod_spec, vec_spec,
            pl.BlockSpec((D_MODEL, D_MODEL), lambda m, n: (0, n)),
            vec_spec, vec_spec,
            pl.BlockSpec((256, 256), lambda m, n: (0, 0)),
            pl.BlockSpec((CONV_K, D_MODEL), lambda m, n: (0, 0)),
        ],
        out_specs=[
            row_spec, row_spec, row_spec,
            pl.BlockSpec((None, D_MODEL, ROW_TILE),
                         lambda m, n: (m // tiles_per_seq, 0, m % tiles_per_seq)),
            row_spec, row_spec, row_spec,
        ],
        out_shape=[bf, bf, bf, jax.ShapeDtypeStruct((batch, D_MODEL, seq), _BF16), bf, bf, bf],
        scratch_shapes=[
            pltpu.VMEM((ROW_TILE, D_MODEL), _BF16),
            pltpu.VMEM((ROW_TILE, D_MODEL), _F32),
            pltpu.VMEM((ROW_TILE, D_MODEL), _F32),
            pltpu.VMEM((2 * SUBLANES, D_MODEL), _F32),
        ],
        compiler_params=_cparams(("arbitrary", "arbitrary")),
        name="inproj",
    )(x2, sc, sh, lng, w_bf, gq, gk, gmat, cw)


def _suffix_products(tot):
    row = lax.broadcasted_iota(jnp.int32, tot.shape, 0)
    incl = tot
    for sh in (1, 2, 4):
        incl = incl * jnp.where(row < SUBLANES - sh, pltpu.roll(incl, SUBLANES - sh, 0), 1.0)
    excl = jnp.where(row < SUBLANES - 1, pltpu.roll(incl, SUBLANES - 1, 0), 1.0)
    total = jnp.broadcast_to(incl[0:1, :], tot.shape)
    return excl, total


def _tree_product(vals):
    vals = list(vals)
    while len(vals) > 1:
        vals = [vals[i] * vals[i + 1] for i in range(0, len(vals), 2)]
    return vals[0]


def _attn_kernel(q0_ref, q1_ref, k_ref, vt_ref, mask_ref, y_ref,
                 zx0, zx1, zy0, zy1, ax0, ax1, ay0, ay1, raw_scr, acc_scr, carry_scr, *, n_qblk):
    n_tiles = n_qblk * (n_qblk + 1) // 2
    q_refs = (q0_ref, q1_ref)

    def tile_start(idx):
        return pl.multiple_of(idx * PERM_TILE, PERM_TILE)

    def advance(tile):
        qi, jj = tile
        last = jj == qi
        return jnp.where(last, qi + 1, qi), jnp.where(last, 0, jj + 1)

    def scores(tile, z_out, h):
        qi = jnp.minimum(tile[0], n_qblk - 1)
        j = qi - tile[1]
        qh = q_refs[h][pl.ds(tile_start(qi), PERM_TILE), :]
        kt = k_ref[pl.ds(tile_start(j), PERM_TILE), :]
        z_out[:, h * PERM_TILE:(h + 1) * PERM_TILE] = lax.dot_general(
            kt, qh, (((1,), (1,)), ((), ())), preferred_element_type=_F32)

    def apply_values(tile, a_in, h):
        qi, jj = tile
        rows = slice(h * SB_HEAD_DIM, (h + 1) * SB_HEAD_DIM)
        vt = vt_ref[rows, pl.ds(tile_start(qi - jj), PERM_TILE)]
        part = jnp.dot(vt, a_in[:, h * PERM_TILE:(h + 1) * PERM_TILE], preferred_element_type=_F32)
        acc_scr[rows, :] = part + jnp.where(jj == 0, 0.0, acc_scr[rows, :])

    def flush(tile):
        y_ref[pl.ds(tile_start(tile[0]), PERM_TILE), :] = acc_scr[...].T.astype(_BF16)

    def scan(tile, z_in, a_out, col):
        diag = tile[1] == 0
        mask_idx = jnp.where(diag, 0, 1)
        lanes = slice(col * 128, (col + 1) * 128)
        mlanes = slice((col % 2) * 128, (col % 2 + 1) * 128)
        span = SEG // SUB_CHAINS
        p = [None] * SUB_CHAINS
        for step in range(span):
            for c in range(SUB_CHAINS):
                i = SEG - 1 - c * span - step
                r = slice(i * SUBLANES, (i + 1) * SUBLANES)
                nb = 1.0 / (1.0 + jnp.exp2(z_in[r, lanes]))
                nb = jnp.maximum(nb, mask_ref[mask_idx, r, mlanes])
                if step == 0:
                    raw_scr[r, lanes] = 1.0 - nb
                    p[c] = nb
                else:
                    pn = p[c] * nb
                    raw_scr[r, lanes] = p[c] - pn
                    p[c] = pn
        tot = _tree_product(p)
        excl, total = _suffix_products(tot)
        carry = jnp.where(diag, 1.0, carry_scr[:, lanes])
        scale = [carry * excl]
        for c in range(1, SUB_CHAINS):
            scale.append(scale[c - 1] * p[c - 1])
        carry_scr[:, lanes] = carry * total
        for i in range(0, SEG, 2):
            pair = [raw_scr[(i + d) * SUBLANES:(i + d + 1) * SUBLANES, lanes]
                    * scale[(SEG - 1 - i - d) // span] for d in range(2)]
            a_out[i * SUBLANES:(i + 2) * SUBLANES, lanes] = jnp.concatenate(
                pair, axis=0).astype(_BF16)

    def block(prv, cur, nxt, z_cur, a_cur, z_nxt, a_prv):
        for kind, slot, idx in BLOCK_ORDER:
            if kind == "c":
                scan(cur[slot], z_cur[slot], a_cur[slot], idx)
            elif kind == "s":
                scores(nxt[slot], z_nxt[slot], idx)
            elif kind == "v":
                apply_values(prv[slot], a_prv[slot], idx)
            else:
                flush(prv[slot])

    zx, zy, ax, ay = (zx0, zx1), (zy0, zy1), (ax0, ax1), (ay0, ay1)
    t0 = (jnp.int32(0), jnp.int32(0))
    t1 = advance(t0)
    acc_scr[...] = jnp.zeros_like(acc_scr)
    for a_ref in ay:
        a_ref[...] = jnp.zeros_like(a_ref)
    for h in range(HEADS_PER_STEP):
        scores(t0, zx0, h)
        scores(t1, zx1, h)

    def body(it, state):
        prv, cur = state
        nxt0 = advance(cur[1])
        nxt = (nxt0, advance(nxt0))

        @pl.when(it % 2 == 0)
        def _():
            block(prv, cur, nxt, zx, ax, zy, ay)

        @pl.when(it % 2 == 1)
        def _():
            block(prv, cur, nxt, zy, ay, zx, ax)

        return cur, nxt

    pending, _ = lax.fori_loop(0, n_tiles // 2, body, ((t0, t0), (t0, t1)))
    for slot in range(2):
        for h in range(HEADS_PER_STEP):
            apply_values(pending[slot], ay[slot], h)
        flush(pending[slot])


def _attention(q0, q1, k3, vt3, mask):
    batch, seq, _ = k3.shape
    n_pairs = SB_HEADS // HEADS_PER_STEP
    lanes = HEADS_PER_STEP * SB_HEAD_DIM
    wide = HEADS_PER_STEP * PERM_TILE
    n_qblk = seq // PERM_TILE
    assert (n_qblk * (n_qblk + 1) // 2) % 4 == 0
    seq_spec = pl.BlockSpec((None, seq, lanes), lambda b, hp: (b, 0, hp))
    return pl.pallas_call(
        functools.partial(_attn_kernel, n_qblk=n_qblk),
        grid=(batch, n_pairs),
        in_specs=[
            seq_spec, seq_spec, seq_spec,
            pl.BlockSpec((None, lanes, seq), lambda b, hp: (b, hp, 0)),
            pl.BlockSpec((2, PERM_TILE, PERM_TILE), lambda b, hp: (0, 0, 0)),
        ],
        out_specs=seq_spec,
        out_shape=jax.ShapeDtypeStruct((batch, seq, D_MODEL), _BF16),
        scratch_shapes=[pltpu.VMEM((PERM_TILE, wide), _F32)] * 4 + [
            pltpu.VMEM((PERM_TILE, wide), _BF16)] * 4 + [
            pltpu.VMEM((PERM_TILE, wide), _F32),
            pltpu.VMEM((lanes, PERM_TILE), _F32),
            pltpu.VMEM((SUBLANES, wide), _F32),
        ],
        compiler_params=_cparams(("arbitrary", "arbitrary")),
        name="sb_attention",
    )(q0, q1, k3, vt3, mask)


def _merge_kernel(ya_ref, yb_ref, sga_ref, sgb_ref, x_ref, wa_ref, wb_ref, wo_ref,
                  g1_ref, sc2_ref, sh2_ref, ln2_ref, x1_ref, h2_ref):
    ta = jnp.dot(ya_ref[...], wa_ref[...], preferred_element_type=_F32)
    tb = jnp.dot(yb_ref[...], wb_ref[...], preferred_element_type=_F32)
    merged = sga_ref[...].astype(_F32) * ta + sgb_ref[...].astype(_F32) * tb
    t = jnp.dot(merged.astype(_BF16), wo_ref[...], preferred_element_type=_F32)
    x1 = x_ref[...] + g1_ref[...] * t
    x1_ref[...] = x1
    h2_ref[...] = _rms_modulate(x1, ln2_ref[...], sc2_ref[...], sh2_ref[...]).astype(_BF16)


def _merge(ya, yb, sga, sgb, x2, wa, wb, wo, g1, sc2, sh2, ln2, seq):
    rows = x2.shape[0]
    tiles_per_seq = seq // ROW_TILE
    row_spec = pl.BlockSpec((ROW_TILE, D_MODEL), lambda m: (m, 0))
    w_spec = pl.BlockSpec((D_MODEL, D_MODEL), lambda m: (0, 0))
    mod_spec = pl.BlockSpec((None, 1, D_MODEL), lambda m: (m // tiles_per_seq, 0, 0))
    return pl.pallas_call(
        _merge_kernel,
        grid=(rows // ROW_TILE,),
        in_specs=[row_spec, row_spec, row_spec, row_spec, row_spec, w_spec, w_spec, w_spec,
                  mod_spec, mod_spec, mod_spec, pl.BlockSpec((1, D_MODEL), lambda m: (0, 0))],
        out_specs=[row_spec, row_spec],
        out_shape=[jax.ShapeDtypeStruct((rows, D_MODEL), _F32),
                   jax.ShapeDtypeStruct((rows, D_MODEL), _BF16)],
        compiler_params=_cparams(("arbitrary",)),
        name="merge_outproj",
    )(ya, yb, sga, sgb, x2, wa, wb, wo, g1, sc2, sh2, ln2)


def _ffn_kernel(h_ref, x_ref, wg_ref, wu_ref, wd_ref, g2_ref, o_ref):
    h = h_ref[...]
    acc = None
    for c in range(FFN_HIDDEN // FFN_CHUNK):
        cols = slice(c * FFN_CHUNK, (c + 1) * FFN_CHUNK)
        g = jnp.dot(h, wg_ref[:, cols], preferred_element_type=_F32)
        u = jnp.dot(h, wu_ref[:, cols], preferred_element_type=_F32)
        a = (g * (1.0 / (1.0 + jnp.exp(-g))) * u).astype(_BF16)
        d = jnp.dot(a, wd_ref[cols, :], preferred_element_type=_F32)
        acc = d if acc is None else acc + d
    o_ref[...] = x_ref[...] + g2_ref[...] * acc


def _ffn(h2, x1, wg, wu, wd, g2, seq):
    rows = x1.shape[0]
    tiles_per_seq = seq // ROW_TILE
    row_spec = pl.BlockSpec((ROW_TILE, D_MODEL), lambda m: (m, 0))
    return pl.pallas_call(
        _ffn_kernel,
        grid=(rows // ROW_TILE,),
        in_specs=[
            row_spec, row_spec,
            pl.BlockSpec((D_MODEL, FFN_HIDDEN), lambda m: (0, 0)),
            pl.BlockSpec((D_MODEL, FFN_HIDDEN), lambda m: (0, 0)),
            pl.BlockSpec((FFN_HIDDEN, D_MODEL), lambda m: (0, 0)),
            pl.BlockSpec((None, 1, D_MODEL), lambda m: (m // tiles_per_seq, 0, 0)),
        ],
        out_specs=row_spec,
        out_shape=jax.ShapeDtypeStruct((rows, D_MODEL), _F32),
        compiler_params=_cparams(("arbitrary",)),
        name="swiglu_ffn",
    )(h2, x1, wg, wu, wd, g2)


def _permute_tokens(x):
    b, s, d = x.shape
    return x.reshape(b, s // PERM_TILE, SUBLANES, SEG, d).transpose(0, 1, 3, 2, 4).reshape(b, s, d)


def _unpermute_tokens(x):
    b, s, d = x.shape
    return x.reshape(b, s // PERM_TILE, SEG, SUBLANES, d).transpose(0, 1, 3, 2, 4).reshape(b, s, d)


def _causal_mask():
    pos = np.arange(PERM_TILE)
    tok = (pos % SUBLANES) * SEG + pos // SUBLANES
    masked = (tok[:, None] >= tok[None, :]).astype(np.float32)
    return jnp.asarray(np.stack([masked, np.zeros_like(masked)]))


def _head_mean_matrix():
    idx = np.arange(256) // SB_HEAD_DIM
    return jnp.asarray((idx[:, None] == idx[None, :]).astype(np.float32) / SB_HEAD_DIM, dtype=_BF16)


def kernel(x, c, ada_w, ada_b, ln1_g, w_in, q_norm_g, k_norm_g, conv_w, w_branch_a, w_branch_b,
           w_out, ln2_g, w_ffn_gate, w_ffn_up, w_ffn_down):
    batch, seq, d = x.shape
    depth = ada_w.shape[0]
    assert d == D_MODEL and seq % ROW_TILE == 0 and ROW_TILE % PERM_TILE == 0
    assert batch <= SUBLANES and w_in.shape[2] == IN_WIDTH

    c_pad = jnp.zeros((SUBLANES, d), _F32).at[:batch].set(c)
    mod = _modulation(c_pad, ada_w, ada_b.reshape(depth, 1, -1))[:, :batch]
    mask = _causal_mask()
    gmat = _head_mean_matrix()
    q_scale = math.log2(math.e) / math.sqrt(SB_HEAD_DIM)

    xp = _permute_tokens(x).reshape(batch * seq, d)
    for l in range(depth):
        sh1, sc1, g1, sh2, sc2, g2 = [
            mod[l, :, i * d:(i + 1) * d].reshape(batch, 1, d) for i in range(6)]
        gq = jnp.tile(q_norm_g[l] * q_scale, SB_HEADS).reshape(1, d)
        gk = jnp.tile(k_norm_g[l], SB_HEADS).reshape(1, d)
        q0, q1, k, vt, yb, sga, sgb = _inproj(
            xp, sc1, sh1, ln1_g[l].reshape(1, d), w_in[l].astype(_BF16), gq, gk, gmat,
            conv_w[l], batch, seq)
        ya = _attention(q0.reshape(batch, seq, d), q1.reshape(batch, seq, d),
                        k.reshape(batch, seq, d), vt, mask)
        x1, h2 = _merge(
            ya.reshape(batch * seq, d), yb, sga, sgb, xp,
            w_branch_a[l].astype(_BF16), w_branch_b[l].astype(_BF16), w_out[l].astype(_BF16),
            g1, sc2, sh2, ln2_g[l].reshape(1, d), seq)
        xp = _ffn(h2, x1, w_ffn_gate[l].astype(_BF16), w_ffn_up[l].astype(_BF16),
                  w_ffn_down[l].astype(_BF16), g2, seq)
    return _unpermute_tokens(xp.reshape(batch, seq, d))
```

```python
import functools
import math

import numpy as np
import jax
import jax.numpy as jnp
from jax import lax
from jax.experimental import pallas as pl
from jax.experimental.pallas import tpu as pltpu

D_MODEL = 1024
SB_HEADS = 16
SB_HEAD_DIM = 64
FFN_HIDDEN = 2816
IN_WIDTH = 8 * D_MODEL
CONV_K = 3
EPS = 1e-6

SUBLANES = 8
PERM_TILE = 256
SEG = PERM_TILE // SUBLANES
HEADS_PER_STEP = 2
SUB_CHAINS = 2
BLOCK_ORDER = (
    ("v", 0, 0), ("c", 0, 0), ("v", 0, 1), ("f", 0, 0), ("c", 0, 1), ("v", 1, 0), ("c", 0, 2),
    ("v", 1, 1), ("f", 1, 0), ("c", 0, 3), ("s", 0, 0), ("c", 1, 0), ("s", 0, 1), ("c", 1, 1),
    ("s", 1, 0), ("c", 1, 2), ("c", 1, 3), ("s", 1, 1))
ROW_TILE = 512
FFN_CHUNK = 1408
VMEM_LIMIT = 56 * 1024 * 1024

_BF16 = jnp.bfloat16
_F32 = jnp.float32


def _cparams(sem):
    return pltpu.CompilerParams(dimension_semantics=sem, vmem_limit_bytes=VMEM_LIMIT)


def _mod_kernel(c_ref, w_ref, b_ref, o_ref):
    c = c_ref[...]
    c_act = c * (1.0 / (1.0 + jnp.exp(-c)))
    o_ref[...] = (
        jnp.dot(c_act.astype(_BF16), w_ref[...].astype(_BF16), preferred_element_type=_F32)
        + b_ref[...]
    )


def _modulation(c_pad, ada_w, ada_b3):
    depth = ada_w.shape[0]
    n_chunks = ada_w.shape[2] // D_MODEL
    return pl.pallas_call(
        _mod_kernel,
        grid=(depth, n_chunks),
        in_specs=[
            pl.BlockSpec((SUBLANES, D_MODEL), lambda l, j: (0, 0)),
            pl.BlockSpec((None, D_MODEL, D_MODEL), lambda l, j: (l, 0, j)),
            pl.BlockSpec((None, 1, D_MODEL), lambda l, j: (l, 0, j)),
        ],
        out_specs=pl.BlockSpec((None, SUBLANES, D_MODEL), lambda l, j: (l, 0, j)),
        out_shape=jax.ShapeDtypeStruct((depth, SUBLANES, ada_w.shape[2]), _F32),
        compiler_params=_cparams(("arbitrary", "arbitrary")),
        name="adaln_mod",
    )(c_pad, ada_w, ada_b3)


def _rms_modulate(xf, g, sc, sh):
    ms = jnp.mean(xf * xf, axis=-1, keepdims=True)
    return (xf * lax.rsqrt(ms + EPS) * g) * (1.0 + sc) + sh


def _head_rms_scale(p, gmat):
    sq = (p * p).astype(_BF16)
    cols = []
    for c in range(D_MODEL // 256):
        cols.append(jnp.dot(sq[:, c * 256:(c + 1) * 256], gmat, preferred_element_type=_F32))
    ms = jnp.concatenate(cols, axis=1)
    return lax.rsqrt(ms + EPS)


def _permuted_causal_conv(u, cw_ref, prev_tail):
    w0 = cw_ref[0:1, :]
    w1 = cw_ref[1:2, :]
    w2 = cw_ref[2:3, :]
    sub = lax.broadcasted_iota(jnp.int32, (SUBLANES, u.shape[1]), 0)
    outs = []
    for a in range(u.shape[0] // PERM_TILE):
        t = u[a * PERM_TILE:(a + 1) * PERM_TILE]
        last = PERM_TILE - SUBLANES
        s31 = jnp.where(sub == 0, pltpu.roll(prev_tail[SUBLANES:], 1, 0), pltpu.roll(t[last:], 1, 0))
        s30 = jnp.where(sub == 0, pltpu.roll(prev_tail[:SUBLANES], 1, 0),
                        pltpu.roll(t[last - SUBLANES:last], 1, 0))
        prev1 = jnp.concatenate([s31, t[:last]], axis=0)
        prev2 = jnp.concatenate([s30, s31, t[:last - SUBLANES]], axis=0)
        outs.append(w2 * t + w1 * prev1 + w0 * prev2)
        prev_tail = t[last - SUBLANES:]
    return jnp.concatenate(outs, axis=0), prev_tail


def _inproj_kernel(x_ref, sc_ref, sh_ref, lng_ref, w_ref, gq_ref, gk_ref, gmat_ref, cw_ref,
                   q0_ref, q1_ref, k_ref, vt_ref, yb_ref, sga_ref, sgb_ref, tail_scr, *,
                   tiles_per_seq):
    @pl.when(pl.program_id(0) % tiles_per_seq == 0)
    def _():
        tail_scr[...] = jnp.zeros_like(tail_scr)

    h = _rms_modulate(x_ref[...], lng_ref[...], sc_ref[...], sh_ref[...]).astype(_BF16)

    def proj(n):
        return jnp.dot(h, w_ref[:, n * D_MODEL:(n + 1) * D_MODEL], preferred_element_type=_F32)

    p = proj(0)
    qn = p * _head_rms_scale(p, gmat_ref[...]) * gq_ref[...]
    lane = lax.broadcasted_iota(jnp.int32, qn.shape, 1)
    even = (lane % (2 * SB_HEAD_DIM)) < SB_HEAD_DIM
    q0_ref[...] = jnp.where(even, qn, 0.0).astype(_BF16)
    q1_ref[...] = jnp.where(even, 0.0, qn).astype(_BF16)

    p = proj(1)
    k_ref[...] = (p * _head_rms_scale(p, gmat_ref[...]) * gk_ref[...]).astype(_BF16)

    vt_ref[...] = proj(2).T.astype(_BF16)

    cb = proj(3)
    conv, tail = _permuted_causal_conv(proj(4) * proj(5), cw_ref, tail_scr[...])
    tail_scr[...] = tail
    yb_ref[...] = (cb * conv).astype(_BF16)

    sga_ref[...] = (1.0 / (1.0 + jnp.exp(-proj(6)))).astype(_BF16)
    sgb_ref[...] = (1.0 / (1.0 + jnp.exp(-proj(7)))).astype(_BF16)


def _inproj(x2, sc, sh, lng, w_bf, gq, gk, gmat, cw, batch, seq):
    rows = x2.shape[0]
    tiles_per_seq = seq // ROW_TILE
    row_spec = pl.BlockSpec((ROW_TILE, D_MODEL), lambda m: (m, 0))
    mod_spec = pl.BlockSpec((None, 1, D_MODEL), lambda m: (m // tiles_per_seq, 0, 0))
    vec_spec = pl.BlockSpec((1, D_MODEL), lambda m: (0, 0))
    bf = jax.ShapeDtypeStruct((rows, D_MODEL), _BF16)
    return pl.pallas_call(
        functools.partial(_inproj_kernel, tiles_per_seq=tiles_per_seq),
        grid=(rows // ROW_TILE,),
        in_specs=[
            row_spec, mod_spec, mod_spec, vec_spec,
            pl.BlockSpec((D_MODEL, IN_WIDTH), lambda m: (0, 0)),
            vec_spec, vec_spec,
            pl.BlockSpec((256, 256), lambda m: (0, 0)),
            pl.BlockSpec((CONV_K, D_MODEL), lambda m: (0, 0)),
        ],
        out_specs=[
            row_spec, row_spec, row_spec,
            pl.BlockSpec((None, D_MODEL, ROW_TILE),
                         lambda m: (m // tiles_per_seq, 0, m % tiles_per_seq)),
            row_spec, row_spec, row_spec,
        ],
        out_shape=[bf, bf, bf, jax.ShapeDtypeStruct((batch, D_MODEL, seq), _BF16), bf, bf, bf],
        scratch_shapes=[pltpu.VMEM((2 * SUBLANES, D_MODEL), _F32)],
        compiler_params=_cparams(("arbitrary",)),
        name="inproj",
    )(x2, sc, sh, lng, w_bf, gq, gk, gmat, cw)


def _suffix_products(tot):
    row = lax.broadcasted_iota(jnp.int32, tot.shape, 0)
    incl = tot
    for sh in (1, 2, 4):
        incl = incl * jnp.where(row < SUBLANES - sh, pltpu.roll(incl, SUBLANES - sh, 0), 1.0)
    excl = jnp.where(row < SUBLANES - 1, pltpu.roll(incl, SUBLANES - 1, 0), 1.0)
    total = jnp.broadcast_to(incl[0:1, :], tot.shape)
    return excl, total


def _tree_product(vals):
    vals = list(vals)
    while len(vals) > 1:
        vals = [vals[i] * vals[i + 1] for i in range(0, len(vals), 2)]
    return vals[0]


def _attn_kernel(q0_ref, q1_ref, k_ref, vt_ref, mask_ref, y_ref,
                 zx0, zx1, zy0, zy1, ax0, ax1, ay0, ay1, raw_scr, acc_scr, carry_scr, *, n_qblk):
    n_tiles = n_qblk * (n_qblk + 1) // 2
    q_refs = (q0_ref, q1_ref)

    def tile_start(idx):
        return pl.multiple_of(idx * PERM_TILE, PERM_TILE)

    def advance(tile):
        qi, jj = tile
        last = jj == qi
        return jnp.where(last, qi + 1, qi), jnp.where(last, 0, jj + 1)

    def scores(tile, z_out, h):
        qi = jnp.minimum(tile[0], n_qblk - 1)
        j = qi - tile[1]
        qh = q_refs[h][pl.ds(tile_start(qi), PERM_TILE), :]
        kt = k_ref[pl.ds(tile_start(j), PERM_TILE), :]
        z_out[:, h * PERM_TILE:(h + 1) * PERM_TILE] = lax.dot_general(
            kt, qh, (((1,), (1,)), ((), ())), preferred_element_type=_F32)

    def apply_values(tile, a_in, h):
        qi, jj = tile
        rows = slice(h * SB_HEAD_DIM, (h + 1) * SB_HEAD_DIM)
        vt = vt_ref[rows, pl.ds(tile_start(qi - jj), PERM_TILE)]
        part = jnp.dot(vt, a_in[:, h * PERM_TILE:(h + 1) * PERM_TILE], preferred_element_type=_F32)
        acc_scr[rows, :] = part + jnp.where(jj == 0, 0.0, acc_scr[rows, :])

    def flush(tile):
        y_ref[pl.ds(tile_start(tile[0]), PERM_TILE), :] = acc_scr[...].T.astype(_BF16)

    def scan(tile, z_in, a_out, col):
        diag = tile[1] == 0
        mask_idx = jnp.where(diag, 0, 1)
        lanes = slice(col * 128, (col + 1) * 128)
        mlanes = slice((col % 2) * 128, (col % 2 + 1) * 128)
        span = SEG // SUB_CHAINS
        p = [None] * SUB_CHAINS
        for step in range(span):
            for c in range(SUB_CHAINS):
                i = SEG - 1 - c * span - step
                r = slice(i * SUBLANES, (i + 1) * SUBLANES)
                nb = 1.0 / (1.0 + jnp.exp2(z_in[r, lanes]))
                nb = jnp.maximum(nb, mask_ref[mask_idx, r, mlanes])
                if step == 0:
                    raw_scr[r, lanes] = 1.0 - nb
                    p[c] = nb
                else:
                    pn = p[c] * nb
                    raw_scr[r, lanes] = p[c] - pn
                    p[c] = pn
        tot = _tree_product(p)
        excl, total = _suffix_products(tot)
        carry = jnp.where(diag, 1.0, carry_scr[:, lanes])
        scale = [carry * excl]
        for c in range(1, SUB_CHAINS):
            scale.append(scale[c - 1] * p[c - 1])
        carry_scr[:, lanes] = carry * total
        for i in range(0, SEG, 2):
            pair = [raw_scr[(i + d) * SUBLANES:(i + d + 1) * SUBLANES, lanes]
                    * scale[(SEG - 1 - i - d) // span] for d in range(2)]
            a_out[i * SUBLANES:(i + 2) * SUBLANES, lanes] = jnp.concatenate(
                pair, axis=0).astype(_BF16)

    def block(prv, cur, nxt, z_cur, a_cur, z_nxt, a_prv):
        for kind, slot, idx in BLOCK_ORDER:
            if kind == "c":
                scan(cur[slot], z_cur[slot], a_cur[slot], idx)
            elif kind == "s":
                scores(nxt[slot], z_nxt[slot], idx)
            elif kind == "v":
                apply_values(prv[slot], a_prv[slot], idx)
            else:
                flush(prv[slot])

    zx, zy, ax, ay = (zx0, zx1), (zy0, zy1), (ax0, ax1), (ay0, ay1)
    t0 = (jnp.int32(0), jnp.int32(0))
    t1 = advance(t0)
    acc_scr[...] = jnp.zeros_like(acc_scr)
    for a_ref in ay:
        a_ref[...] = jnp.zeros_like(a_ref)
    for h in range(HEADS_PER_STEP):
        scores(t0, zx0, h)
        scores(t1, zx1, h)

    def body(it, state):
        prv, cur = state
        nxt0 = advance(cur[1])
        nxt = (nxt0, advance(nxt0))

        @pl.when(it % 2 == 0)
        def _():
            block(prv, cur, nxt, zx, ax, zy, ay)

        @pl.when(it % 2 == 1)
        def _():
            block(prv, cur, nxt, zy, ay, zx, ax)

        return cur, nxt

    pending, _ = lax.fori_loop(0, n_tiles // 2, body, ((t0, t0), (t0, t1)))
    for slot in range(2):
        for h in range(HEADS_PER_STEP):
            apply_values(pending[slot], ay[slot], h)
        flush(pending[slot])


def _attention(q0, q1, k3, vt3, mask):
    batch, seq, _ = k3.shape
    n_pairs = SB_HEADS // HEADS_PER_STEP
    lanes = HEADS_PER_STEP * SB_HEAD_DIM
    wide = HEADS_PER_STEP * PERM_TILE
    n_qblk = seq // PERM_TILE
    assert (n_qblk * (n_qblk + 1) // 2) % 4 == 0
    seq_spec = pl.BlockSpec((None, seq, lanes), lambda b, hp: (b, 0, hp))
    return pl.pallas_call(
        functools.partial(_attn_kernel, n_qblk=n_qblk),
        grid=(batch, n_pairs),
        in_specs=[
            seq_spec, seq_spec, seq_spec,
            pl.BlockSpec((None, lanes, seq), lambda b, hp: (b, hp, 0)),
            pl.BlockSpec((2, PERM_TILE, PERM_TILE), lambda b, hp: (0, 0, 0)),
        ],
        out_specs=seq_spec,
        out_shape=jax.ShapeDtypeStruct((batch, seq, D_MODEL), _BF16),
        scratch_shapes=[pltpu.VMEM((PERM_TILE, wide), _F32)] * 4 + [
            pltpu.VMEM((PERM_TILE, wide), _BF16)] * 4 + [
            pltpu.VMEM((PERM_TILE, wide), _F32),
            pltpu.VMEM((lanes, PERM_TILE), _F32),
            pltpu.VMEM((SUBLANES, wide), _F32),
        ],
        compiler_params=_cparams(("arbitrary", "arbitrary")),
        name="sb_attention",
    )(q0, q1, k3, vt3, mask)


def _merge_kernel(ya_ref, yb_ref, sga_ref, sgb_ref, x_ref, wa_ref, wb_ref, wo_ref,
                  g1_ref, sc2_ref, sh2_ref, ln2_ref, x1_ref, h2_ref):
    ta = jnp.dot(ya_ref[...], wa_ref[...], preferred_element_type=_F32)
    tb = jnp.dot(yb_ref[...], wb_ref[...], preferred_element_type=_F32)
    merged = sga_ref[...].astype(_F32) * ta + sgb_ref[...].astype(_F32) * tb
    t = jnp.dot(merged.astype(_BF16), wo_ref[...], preferred_element_type=_F32)
    x1 = x_ref[...] + g1_ref[...] * t
    x1_ref[...] = x1
    h2_ref[...] = _rms_modulate(x1, ln2_ref[...], sc2_ref[...], sh2_ref[...]).astype(_BF16)


def _merge(ya, yb, sga, sgb, x2, wa, wb, wo, g1, sc2, sh2, ln2, seq):
    rows = x2.shape[0]
    tiles_per_seq = seq // ROW_TILE
    row_spec = pl.BlockSpec((ROW_TILE, D_MODEL), lambda m: (m, 0))
    w_spec = pl.BlockSpec((D_MODEL, D_MODEL), lambda m: (0, 0))
    mod_spec = pl.BlockSpec((None, 1, D_MODEL), lambda m: (m // tiles_per_seq, 0, 0))
    return pl.pallas_call(
        _merge_kernel,
        grid=(rows // ROW_TILE,),
        in_specs=[row_spec, row_spec, row_spec, row_spec, row_spec, w_spec, w_spec, w_spec,
                  mod_spec, mod_spec, mod_spec, pl.BlockSpec((1, D_MODEL), lambda m: (0, 0))],
        out_specs=[row_spec, row_spec],
        out_shape=[jax.ShapeDtypeStruct((rows, D_MODEL), _F32),
                   jax.ShapeDtypeStruct((rows, D_MODEL), _BF16)],
        compiler_params=_cparams(("arbitrary",)),
        name="merge_outproj",
    )(ya, yb, sga, sgb, x2, wa, wb, wo, g1, sc2, sh2, ln2)


def _ffn_kernel(h_ref, x_ref, wg_ref, wu_ref, wd_ref, g2_ref, o_ref):
    h = h_ref[...]
    acc = None
    for c in range(FFN_HIDDEN // FFN_CHUNK):
        cols = slice(c * FFN_CHUNK, (c + 1) * FFN_CHUNK)
        g = jnp.dot(h, wg_ref[:, cols], preferred_element_type=_F32)
        u = jnp.dot(h, wu_ref[:, cols], preferred_element_type=_F32)
        a = (g * (1.0 / (1.0 + jnp.exp(-g))) * u).astype(_BF16)
        d = jnp.dot(a, wd_ref[cols, :], preferred_element_type=_F32)
        acc = d if acc is None else acc + d
    o_ref[...] = x_ref[...] + g2_ref[...] * acc


def _ffn(h2, x1, wg, wu, wd, g2, seq):
    rows = x1.shape[0]
    tiles_per_seq = seq // ROW_TILE
    row_spec = pl.BlockSpec((ROW_TILE, D_MODEL), lambda m: (m, 0))
    return pl.pallas_call(
        _ffn_kernel,
        grid=(rows // ROW_TILE,),
        in_specs=[
            row_spec, row_spec,
            pl.BlockSpec((D_MODEL, FFN_HIDDEN), lambda m: (0, 0)),
            pl.BlockSpec((D_MODEL, FFN_HIDDEN), lambda m: (0, 0)),
            pl.BlockSpec((FFN_HIDDEN, D_MODEL), lambda m: (0, 0)),
            pl.BlockSpec((None, 1, D_MODEL), lambda m: (m // tiles_per_seq, 0, 0)),
        ],
        out_specs=row_spec,
        out_shape=jax.ShapeDtypeStruct((rows, D_MODEL), _F32),
        compiler_params=_cparams(("arbitrary",)),
        name="swiglu_ffn",
    )(h2, x1, wg, wu, wd, g2)


def _permute_tokens(x):
    b, s, d = x.shape
    return x.reshape(b, s // PERM_TILE, SUBLANES, SEG, d).transpose(0, 1, 3, 2, 4).reshape(b, s, d)


def _unpermute_tokens(x):
    b, s, d = x.shape
    return x.reshape(b, s // PERM_TILE, SEG, SUBLANES, d).transpose(0, 1, 3, 2, 4).reshape(b, s, d)


def _causal_mask():
    pos = np.arange(PERM_TILE)
    tok = (pos % SUBLANES) * SEG + pos // SUBLANES
    masked = (tok[:, None] >= tok[None, :]).astype(np.float32)
    return jnp.asarray(np.stack([masked, np.zeros_like(masked)]))


def _head_mean_matrix():
    idx = np.arange(256) // SB_HEAD_DIM
    return jnp.asarray((idx[:, None] == idx[None, :]).astype(np.float32) / SB_HEAD_DIM, dtype=_BF16)


def kernel(x, c, ada_w, ada_b, ln1_g, w_in, q_norm_g, k_norm_g, conv_w, w_branch_a, w_branch_b,
           w_out, ln2_g, w_ffn_gate, w_ffn_up, w_ffn_down):
    batch, seq, d = x.shape
    depth = ada_w.shape[0]
    assert d == D_MODEL and seq % ROW_TILE == 0 and ROW_TILE % PERM_TILE == 0
    assert batch <= SUBLANES and w_in.shape[2] == IN_WIDTH

    c_pad = jnp.zeros((SUBLANES, d), _F32).at[:batch].set(c)
    mod = _modulation(c_pad, ada_w, ada_b.reshape(depth, 1, -1))[:, :batch]
    mask = _causal_mask()
    gmat = _head_mean_matrix()
    q_scale = math.log2(math.e) / math.sqrt(SB_HEAD_DIM)

    xp = _permute_tokens(x).reshape(batch * seq, d)
    for l in range(depth):
        sh1, sc1, g1, sh2, sc2, g2 = [
            mod[l, :, i * d:(i + 1) * d].reshape(batch, 1, d) for i in range(6)]
        gq = jnp.tile(q_norm_g[l] * q_scale, SB_HEADS).reshape(1, d)
        gk = jnp.tile(k_norm_g[l], SB_HEADS).reshape(1, d)
        q0, q1, k, vt, yb, sga, sgb = _inproj(
            xp, sc1, sh1, ln1_g[l].reshape(1, d), w_in[l].astype(_BF16), gq, gk, gmat,
            conv_w[l], batch, seq)
        ya = _attention(q0.reshape(batch, seq, d), q1.reshape(batch, seq, d),
                        k.reshape(batch, seq, d), vt, mask)
        x1, h2 = _merge(
            ya.reshape(batch * seq, d), yb, sga, sgb, xp,
            w_branch_a[l].astype(_BF16), w_branch_b[l].astype(_BF16), w_out[l].astype(_BF16),
            g1, sc2, sh2, ln2_g[l].reshape(1, d), seq)
        xp = _ffn(h2, x1, w_ffn_gate[l].astype(_BF16), w_ffn_up[l].astype(_BF16),
                  w_ffn_down[l].astype(_BF16), g2, seq)
    return _unpermute_tokens(xp.reshape(batch, seq, d))
```

```python
import functools
import math

import numpy as np
import jax
import jax.numpy as jnp
from jax import lax
from jax.experimental import pallas as pl
from jax.experimental.pallas import tpu as pltpu

D_MODEL = 1024
SB_HEADS = 16
SB_HEAD_DIM = 64
FFN_HIDDEN = 2816
IN_WIDTH = 8 * D_MODEL
CONV_K = 3
EPS = 1e-6

SUBLANES = 8
PERM_TILE = 256
SEG = PERM_TILE // SUBLANES
HEADS_PER_STEP = 2
SUB_CHAINS = 2
BLOCK_ORDER = (
    ("v", 0, 0), ("c", 0, 0), ("v", 0, 1), ("f", 0, 0), ("c", 0, 1), ("v", 1, 0), ("c", 0, 2),
    ("v", 1, 1), ("f", 1, 0), ("c", 0, 3), ("s", 0, 0), ("c", 1, 0), ("s", 0, 1), ("c", 1, 1),
    ("s", 1, 0), ("c", 1, 2), ("c", 1, 3), ("s", 1, 1))
ROW_TILE = 512
FFN_CHUNK = 1408
VMEM_LIMIT = 56 * 1024 * 1024

_BF16 = jnp.bfloat16
_F32 = jnp.float32


def _cparams(sem):
    return pltpu.CompilerParams(dimension_semantics=sem, vmem_limit_bytes=VMEM_LIMIT)


def _mod_kernel(c_ref, w_ref, b_ref, o_ref):
    c = c_ref[...]
    c_act = c * (1.0 / (1.0 + jnp.exp(-c)))
    o_ref[...] = (
        jnp.dot(c_act.astype(_BF16), w_ref[...].astype(_BF16), preferred_element_type=_F32)
        + b_ref[...]
    )


def _modulation(c_pad, ada_w, ada_b3):
    depth = ada_w.shape[0]
    n_chunks = ada_w.shape[2] // D_MODEL
    return pl.pallas_call(
        _mod_kernel,
        grid=(depth, n_chunks),
        in_specs=[
            pl.BlockSpec((SUBLANES, D_MODEL), lambda l, j: (0, 0)),
            pl.BlockSpec((None, D_MODEL, D_MODEL), lambda l, j: (l, 0, j)),
            pl.BlockSpec((None, 1, D_MODEL), lambda l, j: (l, 0, j)),
        ],
        out_specs=pl.BlockSpec((None, SUBLANES, D_MODEL), lambda l, j: (l, 0, j)),
        out_shape=jax.ShapeDtypeStruct((depth, SUBLANES, ada_w.shape[2]), _F32),
        compiler_params=_cparams(("arbitrary", "arbitrary")),
        name="adaln_mod",
    )(c_pad, ada_w, ada_b3)


def _rms_modulate(xf, g, sc, sh):
    ms = jnp.mean(xf * xf, axis=-1, keepdims=True)
    return (xf * lax.rsqrt(ms + EPS) * g) * (1.0 + sc) + sh


def _head_rms_scale(p, gmat):
    sq = (p * p).astype(_BF16)
    cols = []
    for c in range(D_MODEL // 256):
        cols.append(jnp.dot(sq[:, c * 256:(c + 1) * 256], gmat, preferred_element_type=_F32))
    ms = jnp.concatenate(cols, axis=1)
    return lax.rsqrt(ms + EPS)


def _permuted_causal_conv(u, cw_ref, prev_tail):
    w0 = cw_ref[0:1, :]
    w1 = cw_ref[1:2, :]
    w2 = cw_ref[2:3, :]
    sub = lax.broadcasted_iota(jnp.int32, (SUBLANES, u.shape[1]), 0)
    outs = []
    for a in range(u.shape[0] // PERM_TILE):
        t = u[a * PERM_TILE:(a + 1) * PERM_TILE]
        last = PERM_TILE - SUBLANES
        s31 = jnp.where(sub == 0, pltpu.roll(prev_tail[SUBLANES:], 1, 0), pltpu.roll(t[last:], 1, 0))
        s30 = jnp.where(sub == 0, pltpu.roll(prev_tail[:SUBLANES], 1, 0),
                        pltpu.roll(t[last - SUBLANES:last], 1, 0))
        prev1 = jnp.concatenate([s31, t[:last]], axis=0)
        prev2 = jnp.concatenate([s30, s31, t[:last - SUBLANES]], axis=0)
        outs.append(w2 * t + w1 * prev1 + w0 * prev2)
        prev_tail = t[last - SUBLANES:]
    return jnp.concatenate(outs, axis=0), prev_tail


def _inproj_kernel(x_ref, sc_ref, sh_ref, lng_ref, w_ref, gq_ref, gk_ref, gmat_ref, cw_ref,
                   q0_ref, q1_ref, k_ref, vt_ref, yb_ref, sga_ref, sgb_ref, tail_scr, *,
                   tiles_per_seq):
    @pl.when(pl.program_id(0) % tiles_per_seq == 0)
    def _():
        tail_scr[...] = jnp.zeros_like(tail_scr)

    h = _rms_modulate(x_ref[...], lng_ref[...], sc_ref[...], sh_ref[...]).astype(_BF16)

    def proj(n):
        return jnp.dot(h, w_ref[:, n * D_MODEL:(n + 1) * D_MODEL], preferred_element_type=_F32)

    p = proj(0)
    qn = p * _head_rms_scale(p, gmat_ref[...]) * gq_ref[...]
    lane = lax.broadcasted_iota(jnp.int32, qn.shape, 1)
    even = (lane % (2 * SB_HEAD_DIM)) < SB_HEAD_DIM
    q0_ref[...] = jnp.where(even, qn, 0.0).astype(_BF16)
    q1_ref[...] = jnp.where(even, 0.0, qn).astype(_BF16)

    p = proj(1)
    k_ref[...] = (p * _head_rms_scale(p, gmat_ref[...]) * gk_ref[...]).astype(_BF16)

    vt_ref[...] = proj(2).T.astype(_BF16)

    cb = proj(3)
    conv, tail = _permuted_causal_conv(proj(4) * proj(5), cw_ref, tail_scr[...])
    tail_scr[...] = tail
    yb_ref[...] = (cb * conv).astype(_BF16)

    sga_ref[...] = (1.0 / (1.0 + jnp.exp(-proj(6)))).astype(_BF16)
    sgb_ref[...] = (1.0 / (1.0 + jnp.exp(-proj(7)))).astype(_BF16)


def _inproj(x2, sc, sh, lng, w_bf, gq, gk, gmat, cw, batch, seq):
    rows = x2.shape[0]
    tiles_per_seq = seq // ROW_TILE
    row_spec = pl.BlockSpec((ROW_TILE, D_MODEL), lambda m: (m, 0))
    mod_spec = pl.BlockSpec((None, 1, D_MODEL), lambda m: (m // tiles_per_seq, 0, 0))
    vec_spec = pl.BlockSpec((1, D_MODEL), lambda m: (0, 0))
    bf = jax.ShapeDtypeStruct((rows, D_MODEL), _BF16)
    return pl.pallas_call(
        functools.partial(_inproj_kernel, tiles_per_seq=tiles_per_seq),
        grid=(rows // ROW_TILE,),
        in_specs=[
            row_spec, mod_spec, mod_spec, vec_spec,
            pl.BlockSpec((D_MODEL, IN_WIDTH), lambda m: (0, 0)),
            vec_spec, vec_spec,
            pl.BlockSpec((256, 256), lambda m: (0, 0)),
            pl.BlockSpec((CONV_K, D_MODEL), lambda m: (0, 0)),
        ],
        out_specs=[
            row_spec, row_spec, row_spec,
            pl.BlockSpec((None, D_MODEL, ROW_TILE),
                         lambda m: (m // tiles_per_seq, 0, m % tiles_per_seq)),
            row_spec, row_spec, row_spec,
        ],
        out_shape=[bf, bf, bf, jax.ShapeDtypeStruct((batch, D_MODEL, seq), _BF16), bf, bf, bf],
        scratch_shapes=[pltpu.VMEM((2 * SUBLANES, D_MODEL), _F32)],
        compiler_params=_cparams(("arbitrary",)),
        name="inproj",
    )(x2, sc, sh, lng, w_bf, gq, gk, gmat, cw)


def _suffix_products(tot):
    row = lax.broadcasted_iota(jnp.int32, tot.shape, 0)
    incl = tot
    for sh in (1, 2, 4):
        incl = incl * jnp.where(row < SUBLANES - sh, pltpu.roll(incl, SUBLANES - sh, 0), 1.0)
    excl = jnp.where(row < SUBLANES - 1, pltpu.roll(incl, SUBLANES - 1, 0), 1.0)
    total = jnp.broadcast_to(incl[0:1, :], tot.shape)
    return excl, total


def _tree_product(vals):
    vals = list(vals)
    while len(vals) > 1:
        vals = [vals[i] * vals[i + 1] for i in range(0, len(vals), 2)]
    return vals[0]


def _attn_kernel(q0_ref, q1_ref, k_ref, vt_ref, mask_ref, y_ref,
                 zx0, zx1, zy0, zy1, ax0, ax1, ay0, ay1, raw_scr, acc_scr, carry_scr, *, n_qblk):
    n_tiles = n_qblk * (n_qblk + 1) // 2
    q_refs = (q0_ref, q1_ref)

    def tile_start(idx):
        return pl.multiple_of(idx * PERM_TILE, PERM_TILE)

    def advance(tile):
        qi, jj = tile
        last = jj == qi
        return jnp.where(last, qi + 1, qi), jnp.where(last, 0, jj + 1)

    def scores(tile, z_out, h):
        qi = jnp.minimum(tile[0], n_qblk - 1)
        j = qi - tile[1]
        qh = q_refs[h][pl.ds(tile_start(qi), PERM_TILE), :]
        kt = k_ref[pl.ds(tile_start(j), PERM_TILE), :]
        z_out[:, h * PERM_TILE:(h + 1) * PERM_TILE] = lax.dot_general(
            kt, qh, (((1,), (1,)), ((), ())), preferred_element_type=_F32)

    def apply_values(tile, a_in, h):
        qi, jj = tile
        rows = slice(h * SB_HEAD_DIM, (h + 1) * SB_HEAD_DIM)
        vt = vt_ref[rows, pl.ds(tile_start(qi - jj), PERM_TILE)]
        part = jnp.dot(vt, a_in[:, h * PERM_TILE:(h + 1) * PERM_TILE], preferred_element_type=_F32)
        acc_scr[rows, :] = part + jnp.where(jj == 0, 0.0, acc_scr[rows, :])

    def flush(tile):
        y_ref[pl.ds(tile_start(tile[0]), PERM_TILE), :] = acc_scr[...].T.astype(_BF16)

    def scan(tile, z_in, a_out, col):
        diag = tile[1] == 0
        mask_idx = jnp.where(diag, 0, 1)
        lanes = slice(col * 128, (col + 1) * 128)
        mlanes = slice((col % 2) * 128, (col % 2 + 1) * 128)
        span = SEG // SUB_CHAINS
        p = [None] * SUB_CHAINS
        hold = [None] * SUB_CHAINS
        for step in range(span):
            for c in range(SUB_CHAINS):
                i = SEG - 1 - c * span - step
                r = slice(i * SUBLANES, (i + 1) * SUBLANES)
                nb = 0.5 - 0.5 * jnp.tanh(z_in[r, lanes])
                nb = jnp.maximum(nb, mask_ref[mask_idx, r, mlanes])
                if step == 0:
                    raw = 1.0 - nb
                    p[c] = nb
                else:
                    pn = p[c] * nb
                    raw = p[c] - pn
                    p[c] = pn
                if i % 2 == 1:
                    hold[c] = raw
                else:
                    raw_scr[i * SUBLANES:(i + 2) * SUBLANES, lanes] = jnp.concatenate(
                        [raw, hold[c]], axis=0).astype(_BF16)
        tot = _tree_product(p)
        excl, total = _suffix_products(tot)
        carry = jnp.where(diag, 1.0, carry_scr[:, lanes])
        scale = [carry * excl]
        for c in range(1, SUB_CHAINS):
            scale.append(scale[c - 1] * p[c - 1])
        carry_scr[:, lanes] = carry * total
        scale = [jnp.concatenate([s, s], axis=0).astype(_BF16) for s in scale]
        for i in range(0, SEG, 2):
            r = slice(i * SUBLANES, (i + 2) * SUBLANES)
            a_out[r, lanes] = raw_scr[r, lanes] * scale[(SEG - 1 - i) // span]

    def block(prv, cur, nxt, z_cur, a_cur, z_nxt, a_prv):
        for kind, slot, idx in BLOCK_ORDER:
            if kind == "c":
                scan(cur[slot], z_cur[slot], a_cur[slot], idx)
            elif kind == "s":
                scores(nxt[slot], z_nxt[slot], idx)
            elif kind == "v":
                apply_values(prv[slot], a_prv[slot], idx)
            else:
                flush(prv[slot])

    zx, zy, ax, ay = (zx0, zx1), (zy0, zy1), (ax0, ax1), (ay0, ay1)
    t0 = (jnp.int32(0), jnp.int32(0))
    t1 = advance(t0)
    acc_scr[...] = jnp.zeros_like(acc_scr)
    for a_ref in ay:
        a_ref[...] = jnp.zeros_like(a_ref)
    for h in range(HEADS_PER_STEP):
        scores(t0, zx0, h)
        scores(t1, zx1, h)

    def body(it, state):
        prv, cur = state
        nxt0 = advance(cur[1])
        nxt = (nxt0, advance(nxt0))

        @pl.when(it % 2 == 0)
        def _():
            block(prv, cur, nxt, zx, ax, zy, ay)

        @pl.when(it % 2 == 1)
        def _():
            block(prv, cur, nxt, zy, ay, zx, ax)

        return cur, nxt

    pending, _ = lax.fori_loop(0, n_tiles // 2, body, ((t0, t0), (t0, t1)))
    for slot in range(2):
        for h in range(HEADS_PER_STEP):
            apply_values(pending[slot], ay[slot], h)
        flush(pending[slot])


def _attention(q0, q1, k3, vt3, mask):
    batch, seq, _ = k3.shape
    n_pairs = SB_HEADS // HEADS_PER_STEP
    lanes = HEADS_PER_STEP * SB_HEAD_DIM
    wide = HEADS_PER_STEP * PERM_TILE
    n_qblk = seq // PERM_TILE
    assert (n_qblk * (n_qblk + 1) // 2) % 4 == 0
    seq_spec = pl.BlockSpec((None, seq, lanes), lambda b, hp: (b, 0, hp))
    return pl.pallas_call(
        functools.partial(_attn_kernel, n_qblk=n_qblk),
        grid=(batch, n_pairs),
        in_specs=[
            seq_spec, seq_spec, seq_spec,
            pl.BlockSpec((None, lanes, seq), lambda b, hp: (b, hp, 0)),
            pl.BlockSpec((2, PERM_TILE, PERM_TILE), lambda b, hp: (0, 0, 0)),
        ],
        out_specs=seq_spec,
        out_shape=jax.ShapeDtypeStruct((batch, seq, D_MODEL), _BF16),
        scratch_shapes=[pltpu.VMEM((PERM_TILE, wide), _F32)] * 4 + [
            pltpu.VMEM((PERM_TILE, wide), _BF16)] * 5 + [
            pltpu.VMEM((lanes, PERM_TILE), _F32),
            pltpu.VMEM((SUBLANES, wide), _F32),
        ],
        compiler_params=_cparams(("arbitrary", "arbitrary")),
        name="sb_attention",
    )(q0, q1, k3, vt3, mask)


def _merge_kernel(ya_ref, yb_ref, sga_ref, sgb_ref, x_ref, wa_ref, wb_ref, wo_ref,
                  g1_ref, sc2_ref, sh2_ref, ln2_ref, x1_ref, h2_ref):
    ta = jnp.dot(ya_ref[...], wa_ref[...], preferred_element_type=_F32)
    tb = jnp.dot(yb_ref[...], wb_ref[...], preferred_element_type=_F32)
    merged = sga_ref[...].astype(_F32) * ta + sgb_ref[...].astype(_F32) * tb
    t = jnp.dot(merged.astype(_BF16), wo_ref[...], preferred_element_type=_F32)
    x1 = x_ref[...] + g1_ref[...] * t
    x1_ref[...] = x1
    h2_ref[...] = _rms_modulate(x1, ln2_ref[...], sc2_ref[...], sh2_ref[...]).astype(_BF16)


def _merge(ya, yb, sga, sgb, x2, wa, wb, wo, g1, sc2, sh2, ln2, seq):
    rows = x2.shape[0]
    tiles_per_seq = seq // ROW_TILE
    row_spec = pl.BlockSpec((ROW_TILE, D_MODEL), lambda m: (m, 0))
    w_spec = pl.BlockSpec((D_MODEL, D_MODEL), lambda m: (0, 0))
    mod_spec = pl.BlockSpec((None, 1, D_MODEL), lambda m: (m // tiles_per_seq, 0, 0))
    return pl.pallas_call(
        _merge_kernel,
        grid=(rows // ROW_TILE,),
        in_specs=[row_spec, row_spec, row_spec, row_spec, row_spec, w_spec, w_spec, w_spec,
                  mod_spec, mod_spec, mod_spec, pl.BlockSpec((1, D_MODEL), lambda m: (0, 0))],
        out_specs=[row_spec, row_spec],
        out_shape=[jax.ShapeDtypeStruct((rows, D_MODEL), _F32),
                   jax.ShapeDtypeStruct((rows, D_MODEL), _BF16)],
        compiler_params=_cparams(("arbitrary",)),
        name="merge_outproj",
    )(ya, yb, sga, sgb, x2, wa, wb, wo, g1, sc2, sh2, ln2)


def _ffn_kernel(h_ref, x_ref, wg_ref, wu_ref, wd_ref, g2_ref, o_ref):
    h = h_ref[...]
    acc = None
    for c in range(FFN_HIDDEN // FFN_CHUNK):
        cols = slice(c * FFN_CHUNK, (c + 1) * FFN_CHUNK)
        g = jnp.dot(h, wg_ref[:, cols], preferred_element_type=_F32)
        u = jnp.dot(h, wu_ref[:, cols], preferred_element_type=_F32)
        a = (g * (1.0 / (1.0 + jnp.exp(-g))) * u).astype(_BF16)
        d = jnp.dot(a, wd_ref[cols, :], preferred_element_type=_F32)
        acc = d if acc is None else acc + d
    o_ref[...] = x_ref[...] + g2_ref[...] * acc


def _ffn(h2, x1, wg, wu, wd, g2, seq):
    rows = x1.shape[0]
    tiles_per_seq = seq // ROW_TILE
    row_spec = pl.BlockSpec((ROW_TILE, D_MODEL), lambda m: (m, 0))
    return pl.pallas_call(
        _ffn_kernel,
        grid=(rows // ROW_TILE,),
        in_specs=[
            row_spec, row_spec,
            pl.BlockSpec((D_MODEL, FFN_HIDDEN), lambda m: (0, 0)),
            pl.BlockSpec((D_MODEL, FFN_HIDDEN), lambda m: (0, 0)),
            pl.BlockSpec((FFN_HIDDEN, D_MODEL), lambda m: (0, 0)),
            pl.BlockSpec((None, 1, D_MODEL), lambda m: (m // tiles_per_seq, 0, 0)),
        ],
        out_specs=row_spec,
        out_shape=jax.ShapeDtypeStruct((rows, D_MODEL), _F32),
        compiler_params=_cparams(("arbitrary",)),
        name="swiglu_ffn",
    )(h2, x1, wg, wu, wd, g2)


def _permute_tokens(x):
    b, s, d = x.shape
    return x.reshape(b, s // PERM_TILE, SUBLANES, SEG, d).transpose(0, 1, 3, 2, 4).reshape(b, s, d)


def _unpermute_tokens(x):
    b, s, d = x.shape
    return x.reshape(b, s // PERM_TILE, SEG, SUBLANES, d).transpose(0, 1, 3, 2, 4).reshape(b, s, d)


def _causal_mask():
    pos = np.arange(PERM_TILE)
    tok = (pos % SUBLANES) * SEG + pos // SUBLANES
    masked = (tok[:, None] >= tok[None, :]).astype(np.float32)
    return jnp.asarray(np.stack([masked, np.zeros_like(masked)]))


def _head_mean_matrix():
    idx = np.arange(256) // SB_HEAD_DIM
    return jnp.asarray((idx[:, None] == idx[None, :]).astype(np.float32) / SB_HEAD_DIM, dtype=_BF16)


def kernel(x, c, ada_w, ada_b, ln1_g, w_in, q_norm_g, k_norm_g, conv_w, w_branch_a, w_branch_b,
           w_out, ln2_g, w_ffn_gate, w_ffn_up, w_ffn_down):
    batch, seq, d = x.shape
    depth = ada_w.shape[0]
    assert d == D_MODEL and seq % ROW_TILE == 0 and ROW_TILE % PERM_TILE == 0
    assert batch <= SUBLANES and w_in.shape[2] == IN_WIDTH

    c_pad = jnp.zeros((SUBLANES, d), _F32).at[:batch].set(c)
    mod = _modulation(c_pad, ada_w, ada_b.reshape(depth, 1, -1))[:, :batch]
    mask = _causal_mask()
    gmat = _head_mean_matrix()
    q_scale = 0.5 / math.sqrt(SB_HEAD_DIM)

    xp = _permute_tokens(x).reshape(batch * seq, d)
    for l in range(depth):
        sh1, sc1, g1, sh2, sc2, g2 = [
            mod[l, :, i * d:(i + 1) * d].reshape(batch, 1, d) for i in range(6)]
        gq = jnp.tile(q_norm_g[l] * q_scale, SB_HEADS).reshape(1, d)
        gk = jnp.tile(k_norm_g[l], SB_HEADS).reshape(1, d)
        q0, q1, k, vt, yb, sga, sgb = _inproj(
            xp, sc1, sh1, ln1_g[l].reshape(1, d), w_in[l].astype(_BF16), gq, gk, gmat,
            conv_w[l], batch, seq)
        ya = _attention(q0.reshape(batch, seq, d), q1.reshape(batch, seq, d),
                        k.reshape(batch, seq, d), vt, mask)
        x1, h2 = _merge(
            ya.reshape(batch * seq, d), yb, sga, sgb, xp,
            w_branch_a[l].astype(_BF16), w_branch_b[l].astype(_BF16), w_out[l].astype(_BF16),
            g1, sc2, sh2, ln2_g[l].reshape(1, d), seq)
        xp = _ffn(h2, x1, w_ffn_gate[l].astype(_BF16), w_ffn_up[l].astype(_BF16),
                  w_ffn_down[l].astype(_BF16), g2, seq)
    return _unpermute_tokens(xp.reshape(batch, seq, d))
```

```python
import functools
import math

import numpy as np
import jax
import jax.numpy as jnp
from jax import lax
from jax.experimental import pallas as pl
from jax.experimental.pallas import tpu as pltpu

D_MODEL = 1024
SB_HEADS = 16
SB_HEAD_DIM = 64
FFN_HIDDEN = 2816
IN_WIDTH = 8 * D_MODEL
CONV_K = 3
EPS = 1e-6

SUBLANES = 8
PERM_TILE = 256
SEG = PERM_TILE // SUBLANES
HEADS_PER_STEP = 2
SUB_CHAINS = 2
BLOCK_ORDER = (
    ("v", 0, 0), ("c", 0, 0), ("v", 0, 1), ("f", 0, 0), ("c", 0, 1), ("v", 1, 0), ("c", 0, 2),
    ("v", 1, 1), ("f", 1, 0), ("c", 0, 3), ("s", 0, 0), ("c", 1, 0), ("s", 0, 1), ("c", 1, 1),
    ("s", 1, 0), ("c", 1, 2), ("c", 1, 3), ("s", 1, 1))
ROW_TILE = 512
FFN_CHUNK = 1408
VMEM_LIMIT = 56 * 1024 * 1024

_BF16 = jnp.bfloat16
_F32 = jnp.float32


def _cparams(sem):
    return pltpu.CompilerParams(dimension_semantics=sem, vmem_limit_bytes=VMEM_LIMIT)


def _mod_kernel(c_ref, w_ref, b_ref, o_ref):
    c = c_ref[...]
    c_act = c * (1.0 / (1.0 + jnp.exp(-c)))
    o_ref[...] = (
        jnp.dot(c_act.astype(_BF16), w_ref[...].astype(_BF16), preferred_element_type=_F32)
        + b_ref[...]
    )


def _modulation(c_pad, ada_w, ada_b3):
    depth = ada_w.shape[0]
    n_chunks = ada_w.shape[2] // D_MODEL
    return pl.pallas_call(
        _mod_kernel,
        grid=(depth, n_chunks),
        in_specs=[
            pl.BlockSpec((SUBLANES, D_MODEL), lambda l, j: (0, 0)),
            pl.BlockSpec((None, D_MODEL, D_MODEL), lambda l, j: (l, 0, j)),
            pl.BlockSpec((None, 1, D_MODEL), lambda l, j: (l, 0, j)),
        ],
        out_specs=pl.BlockSpec((None, SUBLANES, D_MODEL), lambda l, j: (l, 0, j)),
        out_shape=jax.ShapeDtypeStruct((depth, SUBLANES, ada_w.shape[2]), _F32),
        compiler_params=_cparams(("arbitrary", "arbitrary")),
        name="adaln_mod",
    )(c_pad, ada_w, ada_b3)


def _rms_modulate(xf, g, sc, sh):
    ms = jnp.mean(xf * xf, axis=-1, keepdims=True)
    return (xf * lax.rsqrt(ms + EPS) * g) * (1.0 + sc) + sh


def _head_rms_scale(p, gmat):
    sq = (p * p).astype(_BF16)
    cols = []
    for c in range(D_MODEL // 256):
        cols.append(jnp.dot(sq[:, c * 256:(c + 1) * 256], gmat, preferred_element_type=_F32))
    ms = jnp.concatenate(cols, axis=1)
    return lax.rsqrt(ms + EPS)


def _permuted_causal_conv(u, cw_ref, prev_tail):
    w0 = cw_ref[0:1, :]
    w1 = cw_ref[1:2, :]
    w2 = cw_ref[2:3, :]
    sub = lax.broadcasted_iota(jnp.int32, (SUBLANES, u.shape[1]), 0)
    outs = []
    for a in range(u.shape[0] // PERM_TILE):
        t = u[a * PERM_TILE:(a + 1) * PERM_TILE]
        last = PERM_TILE - SUBLANES
        s31 = jnp.where(sub == 0, pltpu.roll(prev_tail[SUBLANES:], 1, 0), pltpu.roll(t[last:], 1, 0))
        s30 = jnp.where(sub == 0, pltpu.roll(prev_tail[:SUBLANES], 1, 0),
                        pltpu.roll(t[last - SUBLANES:last], 1, 0))
        prev1 = jnp.concatenate([s31, t[:last]], axis=0)
        prev2 = jnp.concatenate([s30, s31, t[:last - SUBLANES]], axis=0)
        outs.append(w2 * t + w1 * prev1 + w0 * prev2)
        prev_tail = t[last - SUBLANES:]
    return jnp.concatenate(outs, axis=0), prev_tail


def _inproj_kernel(x_ref, sc_ref, sh_ref, lng_ref, w_ref, gq_ref, gk_ref, gmat_ref, cw_ref,
                   q0_ref, q1_ref, k_ref, vt_ref, yb_ref, sga_ref, sgb_ref, tail_scr, *,
                   tiles_per_seq):
    @pl.when(pl.program_id(0) % tiles_per_seq == 0)
    def _():
        tail_scr[...] = jnp.zeros_like(tail_scr)

    h = _rms_modulate(x_ref[...], lng_ref[...], sc_ref[...], sh_ref[...]).astype(_BF16)

    def proj(n):
        return jnp.dot(h, w_ref[:, n * D_MODEL:(n + 1) * D_MODEL], preferred_element_type=_F32)

    p = proj(0)
    qn = p * _head_rms_scale(p, gmat_ref[...]) * gq_ref[...]
    lane = lax.broadcasted_iota(jnp.int32, qn.shape, 1)
    even = (lane % (2 * SB_HEAD_DIM)) < SB_HEAD_DIM
    q0_ref[...] = jnp.where(even, qn, 0.0).astype(_BF16)
    q1_ref[...] = jnp.where(even, 0.0, qn).astype(_BF16)

    p = proj(1)
    k_ref[...] = (p * _head_rms_scale(p, gmat_ref[...]) * gk_ref[...]).astype(_BF16)

    vt_ref[...] = proj(2).T.astype(_BF16)

    cb = proj(3)
    conv, tail = _permuted_causal_conv(proj(4) * proj(5), cw_ref, tail_scr[...])
    tail_scr[...] = tail
    yb_ref[...] = (cb * conv).astype(_BF16)

    sga_ref[...] = (1.0 / (1.0 + jnp.exp(-proj(6)))).astype(_BF16)
    sgb_ref[...] = (1.0 / (1.0 + jnp.exp(-proj(7)))).astype(_BF16)


def _inproj(x2, sc, sh, lng, w_bf, gq, gk, gmat, cw, batch, seq):
    rows = x2.shape[0]
    tiles_per_seq = seq // ROW_TILE
    row_spec = pl.BlockSpec((ROW_TILE, D_MODEL), lambda m: (m, 0))
    mod_spec = pl.BlockSpec((None, 1, D_MODEL), lambda m: (m // tiles_per_seq, 0, 0))
    vec_spec = pl.BlockSpec((1, D_MODEL), lambda m: (0, 0))
    bf = jax.ShapeDtypeStruct((rows, D_MODEL), _BF16)
    return pl.pallas_call(
        functools.partial(_inproj_kernel, tiles_per_seq=tiles_per_seq),
        grid=(rows // ROW_TILE,),
        in_specs=[
            row_spec, mod_spec, mod_spec, vec_spec,
            pl.BlockSpec((D_MODEL, IN_WIDTH), lambda m: (0, 0)),
            vec_spec, vec_spec,
            pl.BlockSpec((256, 256), lambda m: (0, 0)),
            pl.BlockSpec((CONV_K, D_MODEL), lambda m: (0, 0)),
        ],
        out_specs=[
            row_spec, row_spec, row_spec,
            pl.BlockSpec((None, D_MODEL, ROW_TILE),
                         lambda m: (m // tiles_per_seq, 0, m % tiles_per_seq)),
            row_spec, row_spec, row_spec,
        ],
        out_shape=[bf, bf, bf, jax.ShapeDtypeStruct((batch, D_MODEL, seq), _BF16), bf, bf, bf],
        scratch_shapes=[pltpu.VMEM((2 * SUBLANES, D_MODEL), _F32)],
        compiler_params=_cparams(("arbitrary",)),
        name="inproj",
    )(x2, sc, sh, lng, w_bf, gq, gk, gmat, cw)


def _suffix_products(tot):
    row = lax.broadcasted_iota(jnp.int32, tot.shape, 0)
    incl = tot
    for sh in (1, 2, 4):
        incl = incl * jnp.where(row < SUBLANES - sh, pltpu.roll(incl, SUBLANES - sh, 0), 1.0)
    excl = jnp.where(row < SUBLANES - 1, pltpu.roll(incl, SUBLANES - 1, 0), 1.0)
    total = jnp.broadcast_to(incl[0:1, :], tot.shape)
    return excl, total


def _tree_product(vals):
    vals = list(vals)
    while len(vals) > 1:
        vals = [vals[i] * vals[i + 1] for i in range(0, len(vals), 2)]
    return vals[0]


def _attn_kernel(q0_ref, q1_ref, k_ref, vt_ref, mask_ref, y_ref,
                 zx0, zx1, zy0, zy1, ax0, ax1, ay0, ay1, raw_scr, acc_scr, carry_scr, *, n_qblk):
    q_refs = (q0_ref, q1_ref)

    def tile_start(idx):
        return pl.multiple_of(idx * PERM_TILE, PERM_TILE)

    def advance(tile, dead_q):
        qi, jj, valid = tile
        done = jnp.logical_or(jj == qi, qi == dead_q)
        nqi = jnp.where(done, qi + 1, qi)
        njj = jnp.where(done, 0, jj + 1)
        nvalid = jnp.where(jnp.logical_and(valid > 0, nqi < n_qblk), 1, 0)
        return jnp.where(nvalid > 0, nqi, n_qblk - 1), jnp.where(nvalid > 0, njj, 1), nvalid

    def scores(tile, z_out, h):
        qi = tile[0]
        j = qi - tile[1]
        qh = q_refs[h][pl.ds(tile_start(qi), PERM_TILE), :]
        kt = k_ref[pl.ds(tile_start(j), PERM_TILE), :]
        z_out[:, h * PERM_TILE:(h + 1) * PERM_TILE] = lax.dot_general(
            kt, qh, (((1,), (1,)), ((), ())), preferred_element_type=_F32)

    def apply_values(tile, a_in, h):
        qi, jj, _ = tile
        rows = slice(h * SB_HEAD_DIM, (h + 1) * SB_HEAD_DIM)
        vt = vt_ref[rows, pl.ds(tile_start(qi - jj), PERM_TILE)]
        part = jnp.dot(vt, a_in[:, h * PERM_TILE:(h + 1) * PERM_TILE], preferred_element_type=_F32)
        acc_scr[rows, :] = part + jnp.where(jj == 0, 0.0, acc_scr[rows, :])

    def flush(tile):
        y_ref[pl.ds(tile_start(tile[0]), PERM_TILE), :] = acc_scr[...].T.astype(_BF16)

    def scan(tile, z_in, a_out, col):
        diag = tile[1] == 0
        mask_idx = jnp.where(diag, 0, 1)
        lanes = slice(col * 128, (col + 1) * 128)
        mlanes = slice((col % 2) * 128, (col % 2 + 1) * 128)
        span = SEG // SUB_CHAINS
        p = [None] * SUB_CHAINS
        for step in range(span):
            for c in range(SUB_CHAINS):
                i = SEG - 1 - c * span - step
                r = slice(i * SUBLANES, (i + 1) * SUBLANES)
                nb = 1.0 / (1.0 + jnp.exp2(z_in[r, lanes]))
                nb = jnp.maximum(nb, mask_ref[mask_idx, r, mlanes])
                if step == 0:
                    raw_scr[r, lanes] = 1.0 - nb
                    p[c] = nb
                else:
                    pn = p[c] * nb
                    raw_scr[r, lanes] = p[c] - pn
                    p[c] = pn
        tot = _tree_product(p)
        excl, total = _suffix_products(tot)
        carry = jnp.where(diag, 1.0, carry_scr[:, lanes])
        carry = jnp.where(tile[2] > 0, carry, 0.0)
        scale = [carry * excl]
        for c in range(1, SUB_CHAINS):
            scale.append(scale[c - 1] * p[c - 1])
        carry_scr[:, lanes] = carry * total
        for i in range(0, SEG, 2):
            pair = [raw_scr[(i + d) * SUBLANES:(i + d + 1) * SUBLANES, lanes]
                    * scale[(SEG - 1 - i - d) // span] for d in range(2)]
            a_out[i * SUBLANES:(i + 2) * SUBLANES, lanes] = jnp.concatenate(
                pair, axis=0).astype(_BF16)

    def block(prv, cur, nxt, z_cur, a_cur, z_nxt, a_prv):
        for kind, slot, idx in BLOCK_ORDER:
            if kind == "c":
                scan(cur[slot], z_cur[slot], a_cur[slot], idx)
            elif kind == "s":
                scores(nxt[slot], z_nxt[slot], idx)
            elif kind == "v":
                apply_values(prv[slot], a_prv[slot], idx)
            else:
                flush(prv[slot])

    zx, zy, ax, ay = (zx0, zx1), (zy0, zy1), (ax0, ax1), (ay0, ay1)
    no_dead = jnp.int32(-1)
    t0 = (jnp.int32(0), jnp.int32(0), jnp.int32(1))
    t1 = advance(t0, no_dead)
    acc_scr[...] = jnp.zeros_like(acc_scr)
    carry_scr[...] = jnp.ones_like(carry_scr)
    for a_ref in ay:
        a_ref[...] = jnp.zeros_like(a_ref)
    for h in range(HEADS_PER_STEP):
        scores(t0, zx0, h)
        scores(t1, zx1, h)

    def body(state):
        it, prv, cur = state
        alive = jnp.max(carry_scr[...]) > 0.0
        dead_q = jnp.where(alive, no_dead, prv[1][0])
        nxt0 = advance(cur[1], dead_q)
        nxt = (nxt0, advance(nxt0, dead_q))

        @pl.when(it % 2 == 0)
        def _():
            block(prv, cur, nxt, zx, ax, zy, ay)

        @pl.when(it % 2 == 1)
        def _():
            block(prv, cur, nxt, zy, ay, zx, ax)

        return it + 1, cur, nxt

    n_iter, pending, _ = lax.while_loop(
        lambda state: state[2][0][2] > 0, body, (jnp.int32(0), (t0, t0), (t0, t1)))

    def drain(a_last):
        for slot in range(2):
            for h in range(HEADS_PER_STEP):
                apply_values(pending[slot], a_last[slot], h)
            flush(pending[slot])

    @pl.when(n_iter % 2 == 1)
    def _():
        drain(ax)

    @pl.when(n_iter % 2 == 0)
    def _():
        drain(ay)


def _attention(q0, q1, k3, vt3, mask):
    batch, seq, _ = k3.shape
    n_pairs = SB_HEADS // HEADS_PER_STEP
    lanes = HEADS_PER_STEP * SB_HEAD_DIM
    wide = HEADS_PER_STEP * PERM_TILE
    n_qblk = seq // PERM_TILE
    seq_spec = pl.BlockSpec((None, seq, lanes), lambda b, hp: (b, 0, hp))
    return pl.pallas_call(
        functools.partial(_attn_kernel, n_qblk=n_qblk),
        grid=(batch, n_pairs),
        in_specs=[
            seq_spec, seq_spec, seq_spec,
            pl.BlockSpec((None, lanes, seq), lambda b, hp: (b, hp, 0)),
            pl.BlockSpec((2, PERM_TILE, PERM_TILE), lambda b, hp: (0, 0, 0)),
        ],
        out_specs=seq_spec,
        out_shape=jax.ShapeDtypeStruct((batch, seq, D_MODEL), _BF16),
        scratch_shapes=[pltpu.VMEM((PERM_TILE, wide), _F32)] * 4 + [
            pltpu.VMEM((PERM_TILE, wide), _BF16)] * 4 + [
            pltpu.VMEM((PERM_TILE, wide), _F32),
            pltpu.VMEM((lanes, PERM_TILE), _F32),
            pltpu.VMEM((SUBLANES, wide), _F32),
        ],
        compiler_params=_cparams(("arbitrary", "arbitrary")),
        name="sb_attention",
    )(q0, q1, k3, vt3, mask)


def _merge_kernel(ya_ref, yb_ref, sga_ref, sgb_ref, x_ref, wa_ref, wb_ref, wo_ref,
                  g1_ref, sc2_ref, sh2_ref, ln2_ref, x1_ref, h2_ref):
    ta = jnp.dot(ya_ref[...], wa_ref[...], preferred_element_type=_F32)
    tb = jnp.dot(yb_ref[...], wb_ref[...], preferred_element_type=_F32)
    merged = sga_ref[...].astype(_F32) * ta + sgb_ref[...].astype(_F32) * tb
    t = jnp.dot(merged.astype(_BF16), wo_ref[...], preferred_element_type=_F32)
    x1 = x_ref[...] + g1_ref[...] * t
    x1_ref[...] = x1
    h2_ref[...] = _rms_modulate(x1, ln2_ref[...], sc2_ref[...], sh2_ref[...]).astype(_BF16)


def _merge(ya, yb, sga, sgb, x2, wa, wb, wo, g1, sc2, sh2, ln2, seq):
    rows = x2.shape[0]
    tiles_per_seq = seq // ROW_TILE
    row_spec = pl.BlockSpec((ROW_TILE, D_MODEL), lambda m: (m, 0))
    w_spec = pl.BlockSpec((D_MODEL, D_MODEL), lambda m: (0, 0))
    mod_spec = pl.BlockSpec((None, 1, D_MODEL), lambda m: (m // tiles_per_seq, 0, 0))
    return pl.pallas_call(
        _merge_kernel,
        grid=(rows // ROW_TILE,),
        in_specs=[row_spec, row_spec, row_spec, row_spec, row_spec, w_spec, w_spec, w_spec,
                  mod_spec, mod_spec, mod_spec, pl.BlockSpec((1, D_MODEL), lambda m: (0, 0))],
        out_specs=[row_spec, row_spec],
        out_shape=[jax.ShapeDtypeStruct((rows, D_MODEL), _F32),
                   jax.ShapeDtypeStruct((rows, D_MODEL), _BF16)],
        compiler_params=_cparams(("arbitrary",)),
        name="merge_outproj",
    )(ya, yb, sga, sgb, x2, wa, wb, wo, g1, sc2, sh2, ln2)


def _ffn_kernel(h_ref, x_ref, wg_ref, wu_ref, wd_ref, g2_ref, o_ref):
    h = h_ref[...]
    acc = None
    for c in range(FFN_HIDDEN // FFN_CHUNK):
        cols = slice(c * FFN_CHUNK, (c + 1) * FFN_CHUNK)
        g = jnp.dot(h, wg_ref[:, cols], preferred_element_type=_F32)
        u = jnp.dot(h, wu_ref[:, cols], preferred_element_type=_F32)
        a = (g * (1.0 / (1.0 + jnp.exp(-g))) * u).astype(_BF16)
        d = jnp.dot(a, wd_ref[cols, :], preferred_element_type=_F32)
        acc = d if acc is None else acc + d
    o_ref[...] = x_ref[...] + g2_ref[...] * acc


def _ffn(h2, x1, wg, wu, wd, g2, seq):
    rows = x1.shape[0]
    tiles_per_seq = seq // ROW_TILE
    row_spec = pl.BlockSpec((ROW_TILE, D_MODEL), lambda m: (m, 0))
    return pl.pallas_call(
        _ffn_kernel,
        grid=(rows // ROW_TILE,),
        in_specs=[
            row_spec, row_spec,
            pl.BlockSpec((D_MODEL, FFN_HIDDEN), lambda m: (0, 0)),
            pl.BlockSpec((D_MODEL, FFN_HIDDEN), lambda m: (0, 0)),
            pl.BlockSpec((FFN_HIDDEN, D_MODEL), lambda m: (0, 0)),
            pl.BlockSpec((None, 1, D_MODEL), lambda m: (m // tiles_per_seq, 0, 0)),
        ],
        out_specs=row_spec,
        out_shape=jax.ShapeDtypeStruct((rows, D_MODEL), _F32),
        compiler_params=_cparams(("arbitrary",)),
        name="swiglu_ffn",
    )(h2, x1, wg, wu, wd, g2)


def _permute_tokens(x):
    b, s, d = x.shape
    return x.reshape(b, s // PERM_TILE, SUBLANES, SEG, d).transpose(0, 1, 3, 2, 4).reshape(b, s, d)


def _unpermute_tokens(x):
    b, s, d = x.shape
    return x.reshape(b, s // PERM_TILE, SEG, SUBLANES, d).transpose(0, 1, 3, 2, 4).reshape(b, s, d)


def _causal_mask():
    pos = np.arange(PERM_TILE)
    tok = (pos % SUBLANES) * SEG + pos // SUBLANES
    masked = (tok[:, None] >= tok[None, :]).astype(np.float32)
    return jnp.asarray(np.stack([masked, np.zeros_like(masked)]))


def _head_mean_matrix():
    idx = np.arange(256) // SB_HEAD_DIM
    return jnp.asarray((idx[:, None] == idx[None, :]).astype(np.float32) / SB_HEAD_DIM, dtype=_BF16)


def kernel(x, c, ada_w, ada_b, ln1_g, w_in, q_norm_g, k_norm_g, conv_w, w_branch_a, w_branch_b,
           w_out, ln2_g, w_ffn_gate, w_ffn_up, w_ffn_down):
    batch, seq, d = x.shape
    depth = ada_w.shape[0]
    assert d == D_MODEL and seq % ROW_TILE == 0 and ROW_TILE % PERM_TILE == 0
    assert batch <= SUBLANES and w_in.shape[2] == IN_WIDTH

    c_pad = jnp.zeros((SUBLANES, d), _F32).at[:batch].set(c)
    mod = _modulation(c_pad, ada_w, ada_b.reshape(depth, 1, -1))[:, :batch]
    mask = _causal_mask()
    gmat = _head_mean_matrix()
    q_scale = math.log2(math.e) / math.sqrt(SB_HEAD_DIM)

    xp = _permute_tokens(x).reshape(batch * seq, d)
    for l in range(depth):
        sh1, sc1, g1, sh2, sc2, g2 = [
            mod[l, :, i * d:(i + 1) * d].reshape(batch, 1, d) for i in range(6)]
        gq = jnp.tile(q_norm_g[l] * q_scale, SB_HEADS).reshape(1, d)
        gk = jnp.tile(k_norm_g[l], SB_HEADS).reshape(1, d)
        q0, q1, k, vt, yb, sga, sgb = _inproj(
            xp, sc1, sh1, ln1_g[l].reshape(1, d), w_in[l].astype(_BF16), gq, gk, gmat,
            conv_w[l], batch, seq)
        ya = _attention(q0.reshape(batch, seq, d), q1.reshape(batch, seq, d),
                        k.reshape(batch, seq, d), vt, mask)
        x1, h2 = _merge(
            ya.reshape(batch * seq, d), yb, sga, sgb, xp,
            w_branch_a[l].astype(_BF16), w_branch_b[l].astype(_BF16), w_out[l].astype(_BF16),
            g1, sc2, sh2, ln2_g[l].reshape(1, d), seq)
        xp = _ffn(h2, x1, w_ffn_gate[l].astype(_BF16), w_ffn_up[l].astype(_BF16),
                  w_ffn_down[l].astype(_BF16), g2, seq)
    return _unpermute_tokens(xp.reshape(batch, seq, d))
```

```python
import functools
import math

import numpy as np
import jax
import jax.numpy as jnp
from jax import lax
from jax.experimental import pallas as pl
from jax.experimental.pallas import tpu as pltpu

D_MODEL = 1024
SB_HEADS = 16
SB_HEAD_DIM = 64
FFN_HIDDEN = 2816
IN_WIDTH = 8 * D_MODEL
CONV_K = 3
EPS = 1e-6

SUBLANES = 8
PERM_TILE = 256
SEG = PERM_TILE // SUBLANES
HEADS_PER_STEP = 2
SUB_CHAINS = 2
BLOCK_ORDER = (
    ("v", 0, 0), ("c", 0, 0), ("v", 0, 1), ("f", 0, 0), ("c", 0, 1), ("v", 1, 0), ("c", 0, 2),
    ("v", 1, 1), ("f", 1, 0), ("c", 0, 3), ("s", 0, 0), ("c", 1, 0), ("s", 0, 1), ("c", 1, 1),
    ("s", 1, 0), ("c", 1, 2), ("c", 1, 3), ("s", 1, 1))
ROW_TILE = 512
FFN_CHUNK = 1408
VMEM_LIMIT = 56 * 1024 * 1024

_BF16 = jnp.bfloat16
_F32 = jnp.float32


def _cparams(sem):
    return pltpu.CompilerParams(dimension_semantics=sem, vmem_limit_bytes=VMEM_LIMIT)


def _mod_kernel(c_ref, w_ref, b_ref, o_ref):
    c = c_ref[...]
    c_act = c * (1.0 / (1.0 + jnp.exp(-c)))
    o_ref[...] = (
        jnp.dot(c_act.astype(_BF16), w_ref[...].astype(_BF16), preferred_element_type=_F32)
        + b_ref[...]
    )


def _modulation(c_pad, ada_w, ada_b3):
    depth = ada_w.shape[0]
    n_chunks = ada_w.shape[2] // D_MODEL
    return pl.pallas_call(
        _mod_kernel,
        grid=(depth, n_chunks),
        in_specs=[
            pl.BlockSpec((SUBLANES, D_MODEL), lambda l, j: (0, 0)),
            pl.BlockSpec((None, D_MODEL, D_MODEL), lambda l, j: (l, 0, j)),
            pl.BlockSpec((None, 1, D_MODEL), lambda l, j: (l, 0, j)),
        ],
        out_specs=pl.BlockSpec((None, SUBLANES, D_MODEL), lambda l, j: (l, 0, j)),
        out_shape=jax.ShapeDtypeStruct((depth, SUBLANES, ada_w.shape[2]), _F32),
        compiler_params=_cparams(("arbitrary", "arbitrary")),
        name="adaln_mod",
    )(c_pad, ada_w, ada_b3)


def _rms_modulate(xf, g, sc, sh):
    ms = jnp.mean(xf * xf, axis=-1, keepdims=True)
    return (xf * lax.rsqrt(ms + EPS) * g) * (1.0 + sc) + sh


def _head_rms_scale(p, gmat):
    sq = (p * p).astype(_BF16)
    cols = []
    for c in range(D_MODEL // 256):
        cols.append(jnp.dot(sq[:, c * 256:(c + 1) * 256], gmat, preferred_element_type=_F32))
    ms = jnp.concatenate(cols, axis=1)
    return lax.rsqrt(ms + EPS)


def _permuted_causal_conv(u, cw_ref, prev_tail):
    w0 = cw_ref[0:1, :]
    w1 = cw_ref[1:2, :]
    w2 = cw_ref[2:3, :]
    sub = lax.broadcasted_iota(jnp.int32, (SUBLANES, u.shape[1]), 0)
    outs = []
    for a in range(u.shape[0] // PERM_TILE):
        t = u[a * PERM_TILE:(a + 1) * PERM_TILE]
        last = PERM_TILE - SUBLANES
        s31 = jnp.where(sub == 0, pltpu.roll(prev_tail[SUBLANES:], 1, 0), pltpu.roll(t[last:], 1, 0))
        s30 = jnp.where(sub == 0, pltpu.roll(prev_tail[:SUBLANES], 1, 0),
                        pltpu.roll(t[last - SUBLANES:last], 1, 0))
        prev1 = jnp.concatenate([s31, t[:last]], axis=0)
        prev2 = jnp.concatenate([s30, s31, t[:last - SUBLANES]], axis=0)
        outs.append(w2 * t + w1 * prev1 + w0 * prev2)
        prev_tail = t[last - SUBLANES:]
    return jnp.concatenate(outs, axis=0), prev_tail


def _inproj_kernel(x_ref, sc_ref, sh_ref, lng_ref, w_ref, gq_ref, gk_ref, gmat_ref, cw_ref,
                   q0_ref, q1_ref, k_ref, vt_ref, yb_ref, sga_ref, sgb_ref, tail_scr, *,
                   tiles_per_seq):
    @pl.when(pl.program_id(0) % tiles_per_seq == 0)
    def _():
        tail_scr[...] = jnp.zeros_like(tail_scr)

    h = _rms_modulate(x_ref[...], lng_ref[...], sc_ref[...], sh_ref[...]).astype(_BF16)

    def proj(n):
        return jnp.dot(h, w_ref[:, n * D_MODEL:(n + 1) * D_MODEL], preferred_element_type=_F32)

    p = proj(0)
    qn = p * _head_rms_scale(p, gmat_ref[...]) * gq_ref[...]
    lane = lax.broadcasted_iota(jnp.int32, qn.shape, 1)
    even = (lane % (2 * SB_HEAD_DIM)) < SB_HEAD_DIM
    q0_ref[...] = jnp.where(even, qn, 0.0).astype(_BF16)
    q1_ref[...] = jnp.where(even, 0.0, qn).astype(_BF16)

    p = proj(1)
    k_ref[...] = (p * _head_rms_scale(p, gmat_ref[...]) * gk_ref[...]).astype(_BF16)

    vt_ref[...] = proj(2).T.astype(_BF16)

    cb = proj(3)
    conv, tail = _permuted_causal_conv(proj(4) * proj(5), cw_ref, tail_scr[...])
    tail_scr[...] = tail
    yb_ref[...] = (cb * conv).astype(_BF16)

    sga_ref[...] = (1.0 / (1.0 + jnp.exp(-proj(6)))).astype(_BF16)
    sgb_ref[...] = (1.0 / (1.0 + jnp.exp(-proj(7)))).astype(_BF16)


def _inproj(x2, sc, sh, lng, w_bf, gq, gk, gmat, cw, batch, seq):
    rows = x2.shape[0]
    tiles_per_seq = seq // ROW_TILE
    row_spec = pl.BlockSpec((ROW_TILE, D_MODEL), lambda m: (m, 0))
    mod_spec = pl.BlockSpec((None, 1, D_MODEL), lambda m: (m // tiles_per_seq, 0, 0))
    vec_spec = pl.BlockSpec((1, D_MODEL), lambda m: (0, 0))
    bf = jax.ShapeDtypeStruct((rows, D_MODEL), _BF16)
    return pl.pallas_call(
        functools.partial(_inproj_kernel, tiles_per_seq=tiles_per_seq),
        grid=(rows // ROW_TILE,),
        in_specs=[
            row_spec, mod_spec, mod_spec, vec_spec,
            pl.BlockSpec((D_MODEL, IN_WIDTH), lambda m: (0, 0)),
            vec_spec, vec_spec,
            pl.BlockSpec((256, 256), lambda m: (0, 0)),
            pl.BlockSpec((CONV_K, D_MODEL), lambda m: (0, 0)),
        ],
        out_specs=[
            row_spec, row_spec, row_spec,
            pl.BlockSpec((None, D_MODEL, ROW_TILE),
                         lambda m: (m // tiles_per_seq, 0, m % tiles_per_seq)),
            row_spec, row_spec, row_spec,
        ],
        out_shape=[bf, bf, bf, jax.ShapeDtypeStruct((batch, D_MODEL, seq), _BF16), bf, bf, bf],
        scratch_shapes=[pltpu.VMEM((2 * SUBLANES, D_MODEL), _F32)],
        compiler_params=_cparams(("arbitrary",)),
        name="inproj",
    )(x2, sc, sh, lng, w_bf, gq, gk, gmat, cw)


def _suffix_products(tot):
    row = lax.broadcasted_iota(jnp.int32, tot.shape, 0)
    incl = tot
    for sh in (1, 2, 4):
        incl = incl * jnp.where(row < SUBLANES - sh, pltpu.roll(incl, SUBLANES - sh, 0), 1.0)
    excl = jnp.where(row < SUBLANES - 1, pltpu.roll(incl, SUBLANES - 1, 0), 1.0)
    total = jnp.broadcast_to(incl[0:1, :], tot.shape)
    return excl, total


def _tree_product(vals):
    vals = list(vals)
    while len(vals) > 1:
        vals = [vals[i] * vals[i + 1] for i in range(0, len(vals), 2)]
    return vals[0]


def _attn_kernel(q0_ref, q1_ref, k_ref, vt_ref, mask_ref, y_ref,
                 zx0, zx1, zy0, zy1, ax0, ax1, ay0, ay1, raw_scr, acc_scr, carry_scr, *, n_qblk):
    q_refs = (q0_ref, q1_ref)

    def tile_start(idx):
        return pl.multiple_of(idx * PERM_TILE, PERM_TILE)

    def advance(tile, dead_q):
        qi, jj, valid = tile
        done = jnp.logical_or(jj == qi, qi == dead_q)
        nqi = jnp.where(done, qi + 1, qi)
        njj = jnp.where(done, 0, jj + 1)
        nvalid = jnp.where(jnp.logical_and(valid > 0, nqi < n_qblk), 1, 0)
        return jnp.where(nvalid > 0, nqi, n_qblk - 1), jnp.where(nvalid > 0, njj, 1), nvalid

    def scores(tile, z_out, h):
        qi = tile[0]
        j = qi - tile[1]
        qh = q_refs[h][pl.ds(tile_start(qi), PERM_TILE), :]
        kt = k_ref[pl.ds(tile_start(j), PERM_TILE), :]
        z_out[:, h * PERM_TILE:(h + 1) * PERM_TILE] = lax.dot_general(
            kt, qh, (((1,), (1,)), ((), ())), preferred_element_type=_F32)

    def apply_values(tile, a_in, h):
        qi, jj, _ = tile
        rows = slice(h * SB_HEAD_DIM, (h + 1) * SB_HEAD_DIM)
        vt = vt_ref[rows, pl.ds(tile_start(qi - jj), PERM_TILE)]
        part = jnp.dot(vt, a_in[:, h * PERM_TILE:(h + 1) * PERM_TILE], preferred_element_type=_F32)
        acc_scr[rows, :] = part + jnp.where(jj == 0, 0.0, acc_scr[rows, :])

    def flush(tile):
        y_ref[pl.ds(tile_start(tile[0]), PERM_TILE), :] = acc_scr[...].T.astype(_BF16)

    def scan(tile, z_in, a_out, col):
        diag = tile[1] == 0
        mask_idx = jnp.where(diag, 0, 1)
        lanes = slice(col * 128, (col + 1) * 128)
        mlanes = slice((col % 2) * 128, (col % 2 + 1) * 128)
        span = SEG // SUB_CHAINS
        p = [None] * SUB_CHAINS
        for step in range(span):
            for c in range(SUB_CHAINS):
                i = SEG - 1 - c * span - step
                r = slice(i * SUBLANES, (i + 1) * SUBLANES)
                nb = 1.0 / (1.0 + jnp.exp2(z_in[r, lanes]))
                nb = jnp.maximum(nb, mask_ref[mask_idx, r, mlanes])
                if step == 0:
                    raw_scr[r, lanes] = 1.0 - nb
                    p[c] = nb
                else:
                    pn = p[c] * nb
                    raw_scr[r, lanes] = p[c] - pn
                    p[c] = pn
        tot = _tree_product(p)
        excl, total = _suffix_products(tot)
        carry = jnp.where(diag, 1.0, carry_scr[:, lanes])
        carry = jnp.where(tile[2] > 0, carry, 0.0)
        scale = [carry * excl]
        for c in range(1, SUB_CHAINS):
            scale.append(scale[c - 1] * p[c - 1])
        carry_scr[:, lanes] = carry * total
        for i in range(0, SEG, 2):
            pair = [raw_scr[(i + d) * SUBLANES:(i + d + 1) * SUBLANES, lanes]
                    * scale[(SEG - 1 - i - d) // span] for d in range(2)]
            a_out[i * SUBLANES:(i + 2) * SUBLANES, lanes] = jnp.concatenate(
                pair, axis=0).astype(_BF16)

    def block(prv, cur, nxt, z_cur, a_cur, z_nxt, a_prv):
        for kind, slot, idx in BLOCK_ORDER:
            if kind == "c":
                scan(cur[slot], z_cur[slot], a_cur[slot], idx)
            elif kind == "s":
                scores(nxt[slot], z_nxt[slot], idx)
            elif kind == "v":
                apply_values(prv[slot], a_prv[slot], idx)
            else:
                flush(prv[slot])

    zx, zy, ax, ay = (zx0, zx1), (zy0, zy1), (ax0, ax1), (ay0, ay1)
    no_dead = jnp.int32(-1)
    t0 = (jnp.int32(0), jnp.int32(0), jnp.int32(1))
    t1 = advance(t0, no_dead)
    acc_scr[...] = jnp.zeros_like(acc_scr)
    carry_scr[...] = jnp.ones_like(carry_scr)
    for a_ref in ay:
        a_ref[...] = jnp.zeros_like(a_ref)
    for h in range(HEADS_PER_STEP):
        scores(t0, zx0, h)
        scores(t1, zx1, h)

    def body(state):
        it, prv, cur = state
        useful = jnp.logical_and(prv[1][0] == cur[1][0], cur[1][1] < cur[1][0])
        dead_q = lax.cond(
            jnp.logical_and(useful, it > 0),
            lambda: jnp.where(jnp.max(carry_scr[...]) > 0.0, no_dead, prv[1][0]),
            lambda: no_dead)
        nxt0 = advance(cur[1], dead_q)
        nxt = (nxt0, advance(nxt0, dead_q))

        @pl.when(it % 2 == 0)
        def _():
            block(prv, cur, nxt, zx, ax, zy, ay)

        @pl.when(it % 2 == 1)
        def _():
            block(prv, cur, nxt, zy, ay, zx, ax)

        return it + 1, cur, nxt

    n_iter, pending, _ = lax.while_loop(
        lambda state: state[2][0][2] > 0, body, (jnp.int32(0), (t0, t0), (t0, t1)))

    def drain(a_last):
        for slot in range(2):
            for h in range(HEADS_PER_STEP):
                apply_values(pending[slot], a_last[slot], h)
            flush(pending[slot])

    @pl.when(n_iter % 2 == 1)
    def _():
        drain(ax)

    @pl.when(n_iter % 2 == 0)
    def _():
        drain(ay)


def _attention(q0, q1, k3, vt3, mask):
    batch, seq, _ = k3.shape
    n_pairs = SB_HEADS // HEADS_PER_STEP
    lanes = HEADS_PER_STEP * SB_HEAD_DIM
    wide = HEADS_PER_STEP * PERM_TILE
    n_qblk = seq // PERM_TILE
    seq_spec = pl.BlockSpec((None, seq, lanes), lambda b, hp: (b, 0, hp))
    return pl.pallas_call(
        functools.partial(_attn_kernel, n_qblk=n_qblk),
        grid=(batch, n_pairs),
        in_specs=[
            seq_spec, seq_spec, seq_spec,
            pl.BlockSpec((None, lanes, seq), lambda b, hp: (b, hp, 0)),
            pl.BlockSpec((2, PERM_TILE, PERM_TILE), lambda b, hp: (0, 0, 0)),
        ],
        out_specs=seq_spec,
        out_shape=jax.ShapeDtypeStruct((batch, seq, D_MODEL), _BF16),
        scratch_shapes=[pltpu.VMEM((PERM_TILE, wide), _F32)] * 4 + [
            pltpu.VMEM((PERM_TILE, wide), _BF16)] * 4 + [
            pltpu.VMEM((PERM_TILE, wide), _F32),
            pltpu.VMEM((lanes, PERM_TILE), _F32),
            pltpu.VMEM((SUBLANES, wide), _F32),
        ],
        compiler_params=_cparams(("arbitrary", "arbitrary")),
        name="sb_attention",
    )(q0, q1, k3, vt3, mask)


def _merge_kernel(ya_ref, yb_ref, sga_ref, sgb_ref, x_ref, wa_ref, wb_ref, wo_ref,
                  g1_ref, sc2_ref, sh2_ref, ln2_ref, x1_ref, h2_ref):
    ta = jnp.dot(ya_ref[...], wa_ref[...], preferred_element_type=_F32)
    tb = jnp.dot(yb_ref[...], wb_ref[...], preferred_element_type=_F32)
    merged = sga_ref[...].astype(_F32) * ta + sgb_ref[...].astype(_F32) * tb
    t = jnp.dot(merged.astype(_BF16), wo_ref[...], preferred_element_type=_F32)
    x1 = x_ref[...] + g1_ref[...] * t
    x1_ref[...] = x1
    h2_ref[...] = _rms_modulate(x1, ln2_ref[...], sc2_ref[...], sh2_ref[...]).astype(_BF16)


def _merge(ya, yb, sga, sgb, x2, wa, wb, wo, g1, sc2, sh2, ln2, seq):
    rows = x2.shape[0]
    tiles_per_seq = seq // ROW_TILE
    row_spec = pl.BlockSpec((ROW_TILE, D_MODEL), lambda m: (m, 0))
    w_spec = pl.BlockSpec((D_MODEL, D_MODEL), lambda m: (0, 0))
    mod_spec = pl.BlockSpec((None, 1, D_MODEL), lambda m: (m // tiles_per_seq, 0, 0))
    return pl.pallas_call(
        _merge_kernel,
        grid=(rows // ROW_TILE,),
        in_specs=[row_spec, row_spec, row_spec, row_spec, row_spec, w_spec, w_spec, w_spec,
                  mod_spec, mod_spec, mod_spec, pl.BlockSpec((1, D_MODEL), lambda m: (0, 0))],
        out_specs=[row_spec, row_spec],
        out_shape=[jax.ShapeDtypeStruct((rows, D_MODEL), _F32),
                   jax.ShapeDtypeStruct((rows, D_MODEL), _BF16)],
        compiler_params=_cparams(("arbitrary",)),
        name="merge_outproj",
    )(ya, yb, sga, sgb, x2, wa, wb, wo, g1, sc2, sh2, ln2)


def _ffn_kernel(h_ref, x_ref, wg_ref, wu_ref, wd_ref, g2_ref, o_ref):
    h = h_ref[...]
    acc = None
    for c in range(FFN_HIDDEN // FFN_CHUNK):
        cols = slice(c * FFN_CHUNK, (c + 1) * FFN_CHUNK)
        g = jnp.dot(h, wg_ref[:, cols], preferred_element_type=_F32)
        u = jnp.dot(h, wu_ref[:, cols], preferred_element_type=_F32)
        a = (g * (1.0 / (1.0 + jnp.exp(-g))) * u).astype(_BF16)
        d = jnp.dot(a, wd_ref[cols, :], preferred_element_type=_F32)
        acc = d if acc is None else acc + d
    o_ref[...] = x_ref[...] + g2_ref[...] * acc


def _ffn(h2, x1, wg, wu, wd, g2, seq):
    rows = x1.shape[0]
    tiles_per_seq = seq // ROW_TILE
    row_spec = pl.BlockSpec((ROW_TILE, D_MODEL), lambda m: (m, 0))
    return pl.pallas_call(
        _ffn_kernel,
        grid=(rows // ROW_TILE,),
        in_specs=[
            row_spec, row_spec,
            pl.BlockSpec((D_MODEL, FFN_HIDDEN), lambda m: (0, 0)),
            pl.BlockSpec((D_MODEL, FFN_HIDDEN), lambda m: (0, 0)),
            pl.BlockSpec((FFN_HIDDEN, D_MODEL), lambda m: (0, 0)),
            pl.BlockSpec((None, 1, D_MODEL), lambda m: (m // tiles_per_seq, 0, 0)),
        ],
        out_specs=row_spec,
        out_shape=jax.ShapeDtypeStruct((rows, D_MODEL), _F32),
        compiler_params=_cparams(("arbitrary",)),
        name="swiglu_ffn",
    )(h2, x1, wg, wu, wd, g2)


def _permute_tokens(x):
    b, s, d = x.shape
    return x.reshape(b, s // PERM_TILE, SUBLANES, SEG, d).transpose(0, 1, 3, 2, 4).reshape(b, s, d)


def _unpermute_tokens(x):
    b, s, d = x.shape
    return x.reshape(b, s // PERM_TILE, SEG, SUBLANES, d).transpose(0, 1, 3, 2, 4).reshape(b, s, d)


def _causal_mask():
    pos = np.arange(PERM_TILE)
    tok = (pos % SUBLANES) * SEG + pos // SUBLANES
    masked = (tok[:, None] >= tok[None, :]).astype(np.float32)
    return jnp.asarray(np.stack([masked, np.zeros_like(masked)]))


def _head_mean_matrix():
    idx = np.arange(256) // SB_HEAD_DIM
    return jnp.asarray((idx[:, None] == idx[None, :]).astype(np.float32) / SB_HEAD_DIM, dtype=_BF16)


def kernel(x, c, ada_w, ada_b, ln1_g, w_in, q_norm_g, k_norm_g, conv_w, w_branch_a, w_branch_b,
           w_out, ln2_g, w_ffn_gate, w_ffn_up, w_ffn_down):
    batch, seq, d = x.shape
    depth = ada_w.shape[0]
    assert d == D_MODEL and seq % ROW_TILE == 0 and ROW_TILE % PERM_TILE == 0
    assert batch <= SUBLANES and w_in.shape[2] == IN_WIDTH

    c_pad = jnp.zeros((SUBLANES, d), _F32).at[:batch].set(c)
    mod = _modulation(c_pad, ada_w, ada_b.reshape(depth, 1, -1))[:, :batch]
    mask = _causal_mask()
    gmat = _head_mean_matrix()
    q_scale = math.log2(math.e) / math.sqrt(SB_HEAD_DIM)

    xp = _permute_tokens(x).reshape(batch * seq, d)
    for l in range(depth):
        sh1, sc1, g1, sh2, sc2, g2 = [
            mod[l, :, i * d:(i + 1) * d].reshape(batch, 1, d) for i in range(6)]
        gq = jnp.tile(q_norm_g[l] * q_scale, SB_HEADS).reshape(1, d)
        gk = jnp.tile(k_norm_g[l], SB_HEADS).reshape(1, d)
        q0, q1, k, vt, yb, sga, sgb = _inproj(
            xp, sc1, sh1, ln1_g[l].reshape(1, d), w_in[l].astype(_BF16), gq, gk, gmat,
            conv_w[l], batch, seq)
        ya = _attention(q0.reshape(batch, seq, d), q1.reshape(batch, seq, d),
                        k.reshape(batch, seq, d), vt, mask)
        x1, h2 = _merge(
            ya.reshape(batch * seq, d), yb, sga, sgb, xp,
            w_branch_a[l].astype(_BF16), w_branch_b[l].astype(_BF16), w_out[l].astype(_BF16),
            g1, sc2, sh2, ln2_g[l].reshape(1, d), seq)
        xp = _ffn(h2, x1, w_ffn_gate[l].astype(_BF16), w_ffn_up[l].astype(_BF16),
                  w_ffn_down[l].astype(_BF16), g2, seq)
    return _unpermute_tokens(xp.reshape(batch, seq, d))
```

```python
import functools
import math

import numpy as np
import jax
import jax.numpy as jnp
from jax import lax
from jax.experimental import pallas as pl
from jax.experimental.pallas import tpu as pltpu

D_MODEL = 1024
SB_HEADS = 16
SB_HEAD_DIM = 64
FFN_HIDDEN = 2816
IN_WIDTH = 8 * D_MODEL
CONV_K = 3
EPS = 1e-6

SUBLANES = 8
PERM_TILE = 256
SEG = PERM_TILE // SUBLANES
HEADS_PER_STEP = 2
SUB_CHAINS = 2
BLOCK_ORDER = (
    ("v", 0, 0), ("c", 0, 0), ("v", 0, 1), ("f", 0, 0), ("c", 0, 1), ("v", 1, 0), ("c", 0, 2),
    ("v", 1, 1), ("f", 1, 0), ("c", 0, 3), ("s", 0, 0), ("c", 1, 0), ("s", 0, 1), ("c", 1, 1),
    ("s", 1, 0), ("c", 1, 2), ("c", 1, 3), ("s", 1, 1))
ROW_TILE = 512
FFN_CHUNK = 1408
VMEM_LIMIT = 56 * 1024 * 1024

_BF16 = jnp.bfloat16
_F32 = jnp.float32


def _cparams(sem):
    return pltpu.CompilerParams(dimension_semantics=sem, vmem_limit_bytes=VMEM_LIMIT)


def _mod_kernel(c_ref, w_ref, b_ref, o_ref):
    c = c_ref[...]
    c_act = c * (1.0 / (1.0 + jnp.exp(-c)))
    o_ref[...] = (
        jnp.dot(c_act.astype(_BF16), w_ref[...].astype(_BF16), preferred_element_type=_F32)
        + b_ref[...]
    )


def _modulation(c_pad, ada_w, ada_b3):
    depth = ada_w.shape[0]
    n_chunks = ada_w.shape[2] // D_MODEL
    return pl.pallas_call(
        _mod_kernel,
        grid=(depth, n_chunks),
        in_specs=[
            pl.BlockSpec((SUBLANES, D_MODEL), lambda l, j: (0, 0)),
            pl.BlockSpec((None, D_MODEL, D_MODEL), lambda l, j: (l, 0, j)),
            pl.BlockSpec((None, 1, D_MODEL), lambda l, j: (l, 0, j)),
        ],
        out_specs=pl.BlockSpec((None, SUBLANES, D_MODEL), lambda l, j: (l, 0, j)),
        out_shape=jax.ShapeDtypeStruct((depth, SUBLANES, ada_w.shape[2]), _F32),
        compiler_params=_cparams(("arbitrary", "arbitrary")),
        name="adaln_mod",
    )(c_pad, ada_w, ada_b3)


def _rms_modulate(xf, g, sc, sh):
    ms = jnp.mean(xf * xf, axis=-1, keepdims=True)
    return (xf * lax.rsqrt(ms + EPS) * g) * (1.0 + sc) + sh


def _head_rms_scale(p, gmat):
    sq = (p * p).astype(_BF16)
    cols = []
    for c in range(D_MODEL // 256):
        cols.append(jnp.dot(sq[:, c * 256:(c + 1) * 256], gmat, preferred_element_type=_F32))
    ms = jnp.concatenate(cols, axis=1)
    return lax.rsqrt(ms + EPS)


def _permuted_causal_conv(u, cw_ref, prev_tail):
    w0 = cw_ref[0:1, :]
    w1 = cw_ref[1:2, :]
    w2 = cw_ref[2:3, :]
    sub = lax.broadcasted_iota(jnp.int32, (SUBLANES, u.shape[1]), 0)
    outs = []
    for a in range(u.shape[0] // PERM_TILE):
        t = u[a * PERM_TILE:(a + 1) * PERM_TILE]
        last = PERM_TILE - SUBLANES
        s31 = jnp.where(sub == 0, pltpu.roll(prev_tail[SUBLANES:], 1, 0), pltpu.roll(t[last:], 1, 0))
        s30 = jnp.where(sub == 0, pltpu.roll(prev_tail[:SUBLANES], 1, 0),
                        pltpu.roll(t[last - SUBLANES:last], 1, 0))
        prev1 = jnp.concatenate([s31, t[:last]], axis=0)
        prev2 = jnp.concatenate([s30, s31, t[:last - SUBLANES]], axis=0)
        outs.append(w2 * t + w1 * prev1 + w0 * prev2)
        prev_tail = t[last - SUBLANES:]
    return jnp.concatenate(outs, axis=0), prev_tail


def _inproj_kernel(x_ref, sc_ref, sh_ref, lng_ref, w_ref, gq_ref, gk_ref, gmat_ref, cw_ref,
                   q0_ref, q1_ref, k_ref, vt_ref, yb_ref, sga_ref, sgb_ref, tail_scr, *,
                   tiles_per_seq):
    @pl.when(pl.program_id(0) % tiles_per_seq == 0)
    def _():
        tail_scr[...] = jnp.zeros_like(tail_scr)

    h = _rms_modulate(x_ref[...], lng_ref[...], sc_ref[...], sh_ref[...]).astype(_BF16)

    def proj(n):
        return jnp.dot(h, w_ref[:, n * D_MODEL:(n + 1) * D_MODEL], preferred_element_type=_F32)

    p = proj(0)
    qn = p * _head_rms_scale(p, gmat_ref[...]) * gq_ref[...]
    lane = lax.broadcasted_iota(jnp.int32, qn.shape, 1)
    even = (lane % (2 * SB_HEAD_DIM)) < SB_HEAD_DIM
    q0_ref[...] = jnp.where(even, qn, 0.0).astype(_BF16)
    q1_ref[...] = jnp.where(even, 0.0, qn).astype(_BF16)

    p = proj(1)
    k_ref[...] = (p * _head_rms_scale(p, gmat_ref[...]) * gk_ref[...]).astype(_BF16)

    vt_ref[...] = proj(2).T.astype(_BF16)

    cb = proj(3)
    conv, tail = _permuted_causal_conv(proj(4) * proj(5), cw_ref, tail_scr[...])
    tail_scr[...] = tail
    yb_ref[...] = (cb * conv).astype(_BF16)

    sga_ref[...] = (1.0 / (1.0 + jnp.exp(-proj(6)))).astype(_BF16)
    sgb_ref[...] = (1.0 / (1.0 + jnp.exp(-proj(7)))).astype(_BF16)


def _inproj(x2, sc, sh, lng, w_bf, gq, gk, gmat, cw, batch, seq):
    rows = x2.shape[0]
    tiles_per_seq = seq // ROW_TILE
    row_spec = pl.BlockSpec((ROW_TILE, D_MODEL), lambda m: (m, 0))
    mod_spec = pl.BlockSpec((None, 1, D_MODEL), lambda m: (m // tiles_per_seq, 0, 0))
    vec_spec = pl.BlockSpec((1, D_MODEL), lambda m: (0, 0))
    bf = jax.ShapeDtypeStruct((rows, D_MODEL), _BF16)
    return pl.pallas_call(
        functools.partial(_inproj_kernel, tiles_per_seq=tiles_per_seq),
        grid=(rows // ROW_TILE,),
        in_specs=[
            row_spec, mod_spec, mod_spec, vec_spec,
            pl.BlockSpec((D_MODEL, IN_WIDTH), lambda m: (0, 0)),
            vec_spec, vec_spec,
            pl.BlockSpec((256, 256), lambda m: (0, 0)),
            pl.BlockSpec((CONV_K, D_MODEL), lambda m: (0, 0)),
        ],
        out_specs=[
            row_spec, row_spec, row_spec,
            pl.BlockSpec((None, D_MODEL, ROW_TILE),
                         lambda m: (m // tiles_per_seq, 0, m % tiles_per_seq)),
            row_spec, row_spec, row_spec,
        ],
        out_shape=[bf, bf, bf, jax.ShapeDtypeStruct((batch, D_MODEL, seq), _BF16), bf, bf, bf],
        scratch_shapes=[pltpu.VMEM((2 * SUBLANES, D_MODEL), _F32)],
        compiler_params=_cparams(("arbitrary",)),
        name="inproj",
    )(x2, sc, sh, lng, w_bf, gq, gk, gmat, cw)


def _suffix_products(tot):
    row = lax.broadcasted_iota(jnp.int32, tot.shape, 0)
    incl = tot
    for sh in (1, 2, 4):
        incl = incl * jnp.where(row < SUBLANES - sh, pltpu.roll(incl, SUBLANES - sh, 0), 1.0)
    excl = jnp.where(row < SUBLANES - 1, pltpu.roll(incl, SUBLANES - 1, 0), 1.0)
    total = jnp.broadcast_to(incl[0:1, :], tot.shape)
    return excl, total


def _tree_product(vals):
    vals = list(vals)
    while len(vals) > 1:
        vals = [vals[i] * vals[i + 1] for i in range(0, len(vals), 2)]
    return vals[0]


def _attn_kernel(q0_ref, q1_ref, k_ref, vt_ref, mask_ref, y_ref,
                 zx0, zx1, zy0, zy1, ax0, ax1, ay0, ay1, raw_scr, acc_scr, carry_scr, *, n_qblk):
    q_refs = (q0_ref, q1_ref)

    def tile_start(idx):
        return pl.multiple_of(idx * PERM_TILE, PERM_TILE)

    def advance(tile, dead, stream):
        qi, jj, valid = tile
        done = jnp.logical_or(jj == qi, dead)
        nqi = jnp.where(done, qi + 2, qi)
        njj = jnp.where(done, 0, jj + 1)
        nvalid = jnp.where(jnp.logical_and(valid > 0, nqi < n_qblk), 1, 0)
        last_q = n_qblk - 2 + stream
        return jnp.where(nvalid > 0, nqi, last_q), jnp.where(nvalid > 0, njj, 1), nvalid

    def scores(tile, z_out, h):
        qi = tile[0]
        j = qi - tile[1]
        qh = q_refs[h][pl.ds(tile_start(qi), PERM_TILE), :]
        kt = k_ref[pl.ds(tile_start(j), PERM_TILE), :]
        z_out[:, h * PERM_TILE:(h + 1) * PERM_TILE] = lax.dot_general(
            kt, qh, (((1,), (1,)), ((), ())), preferred_element_type=_F32)

    def apply_values(tile, a_in, h, stream):
        qi, jj, _ = tile
        rows = slice(h * SB_HEAD_DIM, (h + 1) * SB_HEAD_DIM)
        vt = vt_ref[rows, pl.ds(tile_start(qi - jj), PERM_TILE)]
        part = jnp.dot(vt, a_in[:, h * PERM_TILE:(h + 1) * PERM_TILE], preferred_element_type=_F32)
        acc_scr[stream, rows, :] = part + jnp.where(jj == 0, 0.0, acc_scr[stream, rows, :])

    def flush(tile, stream):
        y_ref[pl.ds(tile_start(tile[0]), PERM_TILE), :] = acc_scr[stream].T.astype(_BF16)

    def scan(tile, z_in, a_out, col, stream):
        diag = tile[1] == 0
        mask_idx = jnp.where(diag, 0, 1)
        lanes = slice(col * 128, (col + 1) * 128)
        mlanes = slice((col % 2) * 128, (col % 2 + 1) * 128)
        span = SEG // SUB_CHAINS
        p = [None] * SUB_CHAINS
        for step in range(span):
            for c in range(SUB_CHAINS):
                i = SEG - 1 - c * span - step
                r = slice(i * SUBLANES, (i + 1) * SUBLANES)
                nb = 1.0 / (1.0 + jnp.exp2(z_in[r, lanes]))
                nb = jnp.maximum(nb, mask_ref[mask_idx, r, mlanes])
                if step == 0:
                    raw_scr[r, lanes] = 1.0 - nb
                    p[c] = nb
                else:
                    pn = p[c] * nb
                    raw_scr[r, lanes] = p[c] - pn
                    p[c] = pn
        tot = _tree_product(p)
        excl, total = _suffix_products(tot)
        carry = jnp.where(diag, 1.0, carry_scr[stream, :, lanes])
        carry = jnp.where(tile[2] > 0, carry, 0.0)
        scale = [carry * excl]
        for c in range(1, SUB_CHAINS):
            scale.append(scale[c - 1] * p[c - 1])
        carry_scr[stream, :, lanes] = carry * total
        for i in range(0, SEG, 2):
            pair = [raw_scr[(i + d) * SUBLANES:(i + d + 1) * SUBLANES, lanes]
                    * scale[(SEG - 1 - i - d) // span] for d in range(2)]
            a_out[i * SUBLANES:(i + 2) * SUBLANES, lanes] = jnp.concatenate(
                pair, axis=0).astype(_BF16)

    def block(prv, cur, nxt, z_cur, a_cur, z_nxt, a_prv):
        for kind, stream, idx in BLOCK_ORDER:
            if kind == "c":
                scan(cur[stream], z_cur[stream], a_cur[stream], idx, stream)
            elif kind == "s":
                scores(nxt[stream], z_nxt[stream], idx)
            elif kind == "v":
                apply_values(prv[stream], a_prv[stream], idx, stream)
            else:
                flush(prv[stream], stream)

    zx, zy, ax, ay = (zx0, zx1), (zy0, zy1), (ax0, ax1), (ay0, ay1)
    first = tuple((jnp.int32(s), jnp.int32(0), jnp.int32(1)) for s in range(2))
    acc_scr[...] = jnp.zeros_like(acc_scr)
    carry_scr[...] = jnp.ones_like(carry_scr)
    for a_ref in ay:
        a_ref[...] = jnp.zeros_like(a_ref)
    for s in range(2):
        for h in range(HEADS_PER_STEP):
            scores(first[s], zx[s], h)

    def body(state):
        it, prv, cur = state
        nxt = []
        for s in range(2):
            useful = jnp.logical_and(prv[s][0] == cur[s][0], cur[s][1] < cur[s][0])
            dead = lax.cond(
                jnp.logical_and(useful, it > 0),
                lambda s=s: (jnp.max(carry_scr[s]) <= 0.0).astype(jnp.int32),
                lambda: jnp.int32(0))
            nxt.append(advance(cur[s], dead > 0, s))
        nxt = tuple(nxt)

        @pl.when(it % 2 == 0)
        def _():
            block(prv, cur, nxt, zx, ax, zy, ay)

        @pl.when(it % 2 == 1)
        def _():
            block(prv, cur, nxt, zy, ay, zx, ax)

        return it + 1, cur, nxt

    n_iter, pending, _ = lax.while_loop(
        lambda state: state[2][0][2] + state[2][1][2] > 0, body, (jnp.int32(0), first, first))

    def drain(a_last):
        for s in range(2):
            for h in range(HEADS_PER_STEP):
                apply_values(pending[s], a_last[s], h, s)
            flush(pending[s], s)

    @pl.when(n_iter % 2 == 1)
    def _():
        drain(ax)

    @pl.when(n_iter % 2 == 0)
    def _():
        drain(ay)


def _attention(q0, q1, k3, vt3, mask):
    batch, seq, _ = k3.shape
    n_pairs = SB_HEADS // HEADS_PER_STEP
    lanes = HEADS_PER_STEP * SB_HEAD_DIM
    wide = HEADS_PER_STEP * PERM_TILE
    n_qblk = seq // PERM_TILE
    assert n_qblk % 2 == 0 and n_qblk >= 4
    seq_spec = pl.BlockSpec((None, seq, lanes), lambda b, hp: (b, 0, hp))
    return pl.pallas_call(
        functools.partial(_attn_kernel, n_qblk=n_qblk),
        grid=(batch, n_pairs),
        in_specs=[
            seq_spec, seq_spec, seq_spec,
            pl.BlockSpec((None, lanes, seq), lambda b, hp: (b, hp, 0)),
            pl.BlockSpec((2, PERM_TILE, PERM_TILE), lambda b, hp: (0, 0, 0)),
        ],
        out_specs=seq_spec,
        out_shape=jax.ShapeDtypeStruct((batch, seq, D_MODEL), _BF16),
        scratch_shapes=[pltpu.VMEM((PERM_TILE, wide), _F32)] * 4 + [
            pltpu.VMEM((PERM_TILE, wide), _BF16)] * 4 + [
            pltpu.VMEM((PERM_TILE, wide), _F32),
            pltpu.VMEM((2, lanes, PERM_TILE), _F32),
            pltpu.VMEM((2, SUBLANES, wide), _F32),
        ],
        compiler_params=_cparams(("arbitrary", "arbitrary")),
        name="sb_attention",
    )(q0, q1, k3, vt3, mask)


def _merge_kernel(ya_ref, yb_ref, sga_ref, sgb_ref, x_ref, wa_ref, wb_ref, wo_ref,
                  g1_ref, sc2_ref, sh2_ref, ln2_ref, x1_ref, h2_ref):
    ta = jnp.dot(ya_ref[...], wa_ref[...], preferred_element_type=_F32)
    tb = jnp.dot(yb_ref[...], wb_ref[...], preferred_element_type=_F32)
    merged = sga_ref[...].astype(_F32) * ta + sgb_ref[...].astype(_F32) * tb
    t = jnp.dot(merged.astype(_BF16), wo_ref[...], preferred_element_type=_F32)
    x1 = x_ref[...] + g1_ref[...] * t
    x1_ref[...] = x1
    h2_ref[...] = _rms_modulate(x1, ln2_ref[...], sc2_ref[...], sh2_ref[...]).astype(_BF16)


def _merge(ya, yb, sga, sgb, x2, wa, wb, wo, g1, sc2, sh2, ln2, seq):
    rows = x2.shape[0]
    tiles_per_seq = seq // ROW_TILE
    row_spec = pl.BlockSpec((ROW_TILE, D_MODEL), lambda m: (m, 0))
    w_spec = pl.BlockSpec((D_MODEL, D_MODEL), lambda m: (0, 0))
    mod_spec = pl.BlockSpec((None, 1, D_MODEL), lambda m: (m // tiles_per_seq, 0, 0))
    return pl.pallas_call(
        _merge_kernel,
        grid=(rows // ROW_TILE,),
        in_specs=[row_spec, row_spec, row_spec, row_spec, row_spec, w_spec, w_spec, w_spec,
                  mod_spec, mod_spec, mod_spec, pl.BlockSpec((1, D_MODEL), lambda m: (0, 0))],
        out_specs=[row_spec, row_spec],
        out_shape=[jax.ShapeDtypeStruct((rows, D_MODEL), _F32),
                   jax.ShapeDtypeStruct((rows, D_MODEL), _BF16)],
        compiler_params=_cparams(("arbitrary",)),
        name="merge_outproj",
    )(ya, yb, sga, sgb, x2, wa, wb, wo, g1, sc2, sh2, ln2)


def _ffn_kernel(h_ref, x_ref, wg_ref, wu_ref, wd_ref, g2_ref, o_ref):
    h = h_ref[...]
    acc = None
    for c in range(FFN_HIDDEN // FFN_CHUNK):
        cols = slice(c * FFN_CHUNK, (c + 1) * FFN_CHUNK)
        g = jnp.dot(h, wg_ref[:, cols], preferred_element_type=_F32)
        u = jnp.dot(h, wu_ref[:, cols], preferred_element_type=_F32)
        a = (g * (1.0 / (1.0 + jnp.exp(-g))) * u).astype(_BF16)
        d = jnp.dot(a, wd_ref[cols, :], preferred_element_type=_F32)
        acc = d if acc is None else acc + d
    o_ref[...] = x_ref[...] + g2_ref[...] * acc


def _ffn(h2, x1, wg, wu, wd, g2, seq):
    rows = x1.shape[0]
    tiles_per_seq = seq // ROW_TILE
    row_spec = pl.BlockSpec((ROW_TILE, D_MODEL), lambda m: (m, 0))
    return pl.pallas_call(
        _ffn_kernel,
        grid=(rows // ROW_TILE,),
        in_specs=[
            row_spec, row_spec,
            pl.BlockSpec((D_MODEL, FFN_HIDDEN), lambda m: (0, 0)),
            pl.BlockSpec((D_MODEL, FFN_HIDDEN), lambda m: (0, 0)),
            pl.BlockSpec((FFN_HIDDEN, D_MODEL), lambda m: (0, 0)),
            pl.BlockSpec((None, 1, D_MODEL), lambda m: (m // tiles_per_seq, 0, 0)),
        ],
        out_specs=row_spec,
        out_shape=jax.ShapeDtypeStruct((rows, D_MODEL), _F32),
        compiler_params=_cparams(("arbitrary",)),
        name="swiglu_ffn",
    )(h2, x1, wg, wu, wd, g2)


def _permute_tokens(x):
    b, s, d = x.shape
    return x.reshape(b, s // PERM_TILE, SUBLANES, SEG, d).transpose(0, 1, 3, 2, 4).reshape(b, s, d)


def _unpermute_tokens(x):
    b, s, d = x.shape
    return x.reshape(b, s // PERM_TILE, SEG, SUBLANES, d).transpose(0, 1, 3, 2, 4).reshape(b, s, d)


def _causal_mask():
    pos = np.arange(PERM_TILE)
    tok = (pos % SUBLANES) * SEG + pos // SUBLANES
    masked = (tok[:, None] >= tok[None, :]).astype(np.float32)
    return jnp.asarray(np.stack([masked, np.zeros_like(masked)]))


def _head_mean_matrix():
    idx = np.arange(256) // SB_HEAD_DIM
    return jnp.asarray((idx[:, None] == idx[None, :]).astype(np.float32) / SB_HEAD_DIM, dtype=_BF16)


def kernel(x, c, ada_w, ada_b, ln1_g, w_in, q_norm_g, k_norm_g, conv_w, w_branch_a, w_branch_b,
           w_out, ln2_g, w_ffn_gate, w_ffn_up, w_ffn_down):
    batch, seq, d = x.shape
    depth = ada_w.shape[0]
    assert d == D_MODEL and seq % ROW_TILE == 0 and ROW_TILE % PERM_TILE == 0
    assert batch <= SUBLANES and w_in.shape[2] == IN_WIDTH

    c_pad = jnp.zeros((SUBLANES, d), _F32).at[:batch].set(c)
    mod = _modulation(c_pad, ada_w, ada_b.reshape(depth, 1, -1))[:, :batch]
    mask = _causal_mask()
    gmat = _head_mean_matrix()
    q_scale = math.log2(math.e) / math.sqrt(SB_HEAD_DIM)

    xp = _permute_tokens(x).reshape(batch * seq, d)
    for l in range(depth):
        sh1, sc1, g1, sh2, sc2, g2 = [
            mod[l, :, i * d:(i + 1) * d].reshape(batch, 1, d) for i in range(6)]
        gq = jnp.tile(q_norm_g[l] * q_scale, SB_HEADS).reshape(1, d)
        gk = jnp.tile(k_norm_g[l], SB_HEADS).reshape(1, d)
        q0, q1, k, vt, yb, sga, sgb = _inproj(
            xp, sc1, sh1, ln1_g[l].reshape(1, d), w_in[l].astype(_BF16), gq, gk, gmat,
            conv_w[l], batch, seq)
        ya = _attention(q0.reshape(batch, seq, d), q1.reshape(batch, seq, d),
                        k.reshape(batch, seq, d), vt, mask)
        x1, h2 = _merge(
            ya.reshape(batch * seq, d), yb, sga, sgb, xp,
            w_branch_a[l].astype(_BF16), w_branch_b[l].astype(_BF16), w_out[l].astype(_BF16),
            g1, sc2, sh2, ln2_g[l].reshape(1, d), seq)
        xp = _ffn(h2, x1, w_ffn_gate[l].astype(_BF16), w_ffn_up[l].astype(_BF16),
                  w_ffn_down[l].astype(_BF16), g2, seq)
    return _unpermute_tokens(xp.reshape(batch, seq, d))
```

```python
import functools
import math

import numpy as np
import jax
import jax.numpy as jnp
from jax import lax
from jax.experimental import pallas as pl
from jax.experimental.pallas import tpu as pltpu

D_MODEL = 1024
SB_HEADS = 16
SB_HEAD_DIM = 64
FFN_HIDDEN = 2816
IN_WIDTH = 8 * D_MODEL
CONV_K = 3
EPS = 1e-6

SUBLANES = 8
PERM_TILE = 256
SEG = PERM_TILE // SUBLANES
HEADS_PER_STEP = 2
SUB_CHAINS = 2
BLOCK_ORDER = (
    ("v", 0, 0), ("c", 0, 0), ("v", 0, 1), ("f", 0, 0), ("c", 0, 1), ("v", 1, 0), ("c", 0, 2),
    ("v", 1, 1), ("f", 1, 0), ("c", 0, 3), ("s", 0, 0), ("c", 1, 0), ("s", 0, 1), ("c", 1, 1),
    ("s", 1, 0), ("c", 1, 2), ("c", 1, 3), ("s", 1, 1))
ROW_TILE = 512
FFN_CHUNK = 1408
VMEM_LIMIT = 56 * 1024 * 1024

_BF16 = jnp.bfloat16
_F32 = jnp.float32


def _cparams(sem):
    return pltpu.CompilerParams(dimension_semantics=sem, vmem_limit_bytes=VMEM_LIMIT)


def _mod_kernel(c_ref, w_ref, b_ref, o_ref):
    c = c_ref[...]
    c_act = c * (1.0 / (1.0 + jnp.exp(-c)))
    o_ref[...] = (
        jnp.dot(c_act.astype(_BF16), w_ref[...].astype(_BF16), preferred_element_type=_F32)
        + b_ref[...]
    )


def _modulation(c_pad, ada_w, ada_b3):
    depth = ada_w.shape[0]
    n_chunks = ada_w.shape[2] // D_MODEL
    return pl.pallas_call(
        _mod_kernel,
        grid=(depth, n_chunks),
        in_specs=[
            pl.BlockSpec((SUBLANES, D_MODEL), lambda l, j: (0, 0)),
            pl.BlockSpec((None, D_MODEL, D_MODEL), lambda l, j: (l, 0, j)),
            pl.BlockSpec((None, 1, D_MODEL), lambda l, j: (l, 0, j)),
        ],
        out_specs=pl.BlockSpec((None, SUBLANES, D_MODEL), lambda l, j: (l, 0, j)),
        out_shape=jax.ShapeDtypeStruct((depth, SUBLANES, ada_w.shape[2]), _F32),
        compiler_params=_cparams(("arbitrary", "arbitrary")),
        name="adaln_mod",
    )(c_pad, ada_w, ada_b3)


def _rms_modulate(xf, g, sc, sh):
    ms = jnp.mean(xf * xf, axis=-1, keepdims=True)
    return (xf * lax.rsqrt(ms + EPS) * g) * (1.0 + sc) + sh


def _head_rms_scale(p, gmat):
    sq = (p * p).astype(_BF16)
    cols = []
    for c in range(D_MODEL // 256):
        cols.append(jnp.dot(sq[:, c * 256:(c + 1) * 256], gmat, preferred_element_type=_F32))
    ms = jnp.concatenate(cols, axis=1)
    return lax.rsqrt(ms + EPS)


def _permuted_causal_conv(u, cw_ref, prev_tail):
    w0 = cw_ref[0:1, :]
    w1 = cw_ref[1:2, :]
    w2 = cw_ref[2:3, :]
    sub = lax.broadcasted_iota(jnp.int32, (SUBLANES, u.shape[1]), 0)
    outs = []
    for a in range(u.shape[0] // PERM_TILE):
        t = u[a * PERM_TILE:(a + 1) * PERM_TILE]
        last = PERM_TILE - SUBLANES
        s31 = jnp.where(sub == 0, pltpu.roll(prev_tail[SUBLANES:], 1, 0), pltpu.roll(t[last:], 1, 0))
        s30 = jnp.where(sub == 0, pltpu.roll(prev_tail[:SUBLANES], 1, 0),
                        pltpu.roll(t[last - SUBLANES:last], 1, 0))
        prev1 = jnp.concatenate([s31, t[:last]], axis=0)
        prev2 = jnp.concatenate([s30, s31, t[:last - SUBLANES]], axis=0)
        outs.append(w2 * t + w1 * prev1 + w0 * prev2)
        prev_tail = t[last - SUBLANES:]
    return jnp.concatenate(outs, axis=0), prev_tail


def _inproj_kernel(x_ref, sc_ref, sh_ref, lng_ref, w_ref, gq_ref, gk_ref, gmat_ref, cw_ref,
                   q0_ref, q1_ref, k_ref, vt_ref, yb_ref, sga_ref, sgb_ref, tail_scr, *,
                   tiles_per_seq):
    @pl.when(pl.program_id(0) % tiles_per_seq == 0)
    def _():
        tail_scr[...] = jnp.zeros_like(tail_scr)

    h = _rms_modulate(x_ref[...], lng_ref[...], sc_ref[...], sh_ref[...]).astype(_BF16)

    def proj(n):
        return jnp.dot(h, w_ref[:, n * D_MODEL:(n + 1) * D_MODEL], preferred_element_type=_F32)

    p = proj(0)
    qn = p * _head_rms_scale(p, gmat_ref[...]) * gq_ref[...]
    lane = lax.broadcasted_iota(jnp.int32, qn.shape, 1)
    even = (lane % (2 * SB_HEAD_DIM)) < SB_HEAD_DIM
    q0_ref[...] = jnp.where(even, qn, 0.0).astype(_BF16)
    q1_ref[...] = jnp.where(even, 0.0, qn).astype(_BF16)

    p = proj(1)
    k_ref[...] = (p * _head_rms_scale(p, gmat_ref[...]) * gk_ref[...]).astype(_BF16)

    vt_ref[...] = proj(2).T.astype(_BF16)

    cb = proj(3)
    conv, tail = _permuted_causal_conv(proj(4) * proj(5), cw_ref, tail_scr[...])
    tail_scr[...] = tail
    yb_ref[...] = (cb * conv).astype(_BF16)

    sga_ref[...] = (1.0 / (1.0 + jnp.exp(-proj(6)))).astype(_BF16)
    sgb_ref[...] = (1.0 / (1.0 + jnp.exp(-proj(7)))).astype(_BF16)


def _inproj(x2, sc, sh, lng, w_bf, gq, gk, gmat, cw, batch, seq):
    rows = x2.shape[0]
    tiles_per_seq = seq // ROW_TILE
    row_spec = pl.BlockSpec((ROW_TILE, D_MODEL), lambda m: (m, 0))
    mod_spec = pl.BlockSpec((None, 1, D_MODEL), lambda m: (m // tiles_per_seq, 0, 0))
    vec_spec = pl.BlockSpec((1, D_MODEL), lambda m: (0, 0))
    bf = jax.ShapeDtypeStruct((rows, D_MODEL), _BF16)
    return pl.pallas_call(
        functools.partial(_inproj_kernel, tiles_per_seq=tiles_per_seq),
        grid=(rows // ROW_TILE,),
        in_specs=[
            row_spec, mod_spec, mod_spec, vec_spec,
            pl.BlockSpec((D_MODEL, IN_WIDTH), lambda m: (0, 0)),
            vec_spec, vec_spec,
            pl.BlockSpec((256, 256), lambda m: (0, 0)),
            pl.BlockSpec((CONV_K, D_MODEL), lambda m: (0, 0)),
        ],
        out_specs=[
            row_spec, row_spec, row_spec,
            pl.BlockSpec((None, D_MODEL, ROW_TILE),
                         lambda m: (m // tiles_per_seq, 0, m % tiles_per_seq)),
            row_spec, row_spec, row_spec,
        ],
        out_shape=[bf, bf, bf, jax.ShapeDtypeStruct((batch, D_MODEL, seq), _BF16), bf, bf, bf],
        scratch_shapes=[pltpu.VMEM((2 * SUBLANES, D_MODEL), _F32)],
        compiler_params=_cparams(("arbitrary",)),
        name="inproj",
    )(x2, sc, sh, lng, w_bf, gq, gk, gmat, cw)


def _suffix_products(tot):
    row = lax.broadcasted_iota(jnp.int32, tot.shape, 0)
    incl = tot
    for sh in (1, 2, 4):
        incl = incl * jnp.where(row < SUBLANES - sh, pltpu.roll(incl, SUBLANES - sh, 0), 1.0)
    excl = jnp.where(row < SUBLANES - 1, pltpu.roll(incl, SUBLANES - 1, 0), 1.0)
    total = jnp.broadcast_to(incl[0:1, :], tot.shape)
    return excl, total


def _tree_product(vals):
    vals = list(vals)
    while len(vals) > 1:
        vals = [vals[i] * vals[i + 1] for i in range(0, len(vals), 2)]
    return vals[0]


def _attn_kernel(q0_ref, q1_ref, k_ref, vt_ref, mask_ref, y_ref,
                 zx0, zx1, zy0, zy1, ax0, ax1, ay0, ay1, raw_scr, acc_scr, carry_scr, *, n_qblk):
    q_refs = (q0_ref, q1_ref)

    def tile_start(idx):
        return pl.multiple_of(idx * PERM_TILE, PERM_TILE)

    def advance(tile, dead, stream):
        qi, jj, valid = tile
        done = jnp.logical_or(jj == qi, dead)
        nqi = jnp.where(done, qi + 2, qi)
        njj = jnp.where(done, 0, jj + 1)
        nvalid = jnp.where(jnp.logical_and(valid > 0, nqi < n_qblk), 1, 0)
        last_q = n_qblk - 2 + stream
        return jnp.where(nvalid > 0, nqi, last_q), jnp.where(nvalid > 0, njj, 1), nvalid

    def scores(tile, z_out, h):
        qi = tile[0]
        j = qi - tile[1]
        qh = q_refs[h][pl.ds(tile_start(qi), PERM_TILE), :]
        kt = k_ref[pl.ds(tile_start(j), PERM_TILE), :]
        z_out[:, h * PERM_TILE:(h + 1) * PERM_TILE] = lax.dot_general(
            kt, qh, (((1,), (1,)), ((), ())), preferred_element_type=_F32)

    def apply_values(tile, a_in, h, stream):
        qi, jj, _ = tile
        rows = slice(h * SB_HEAD_DIM, (h + 1) * SB_HEAD_DIM)
        vt = vt_ref[rows, pl.ds(tile_start(qi - jj), PERM_TILE)]
        part = jnp.dot(vt, a_in[:, h * PERM_TILE:(h + 1) * PERM_TILE], preferred_element_type=_F32)
        acc_scr[stream, rows, :] = part + jnp.where(jj == 0, 0.0, acc_scr[stream, rows, :])

    def flush(tile, stream):
        y_ref[pl.ds(tile_start(tile[0]), PERM_TILE), :] = acc_scr[stream].T.astype(_BF16)

    def scan(tile, z_in, a_out, col, stream):
        diag = tile[1] == 0
        mask_idx = jnp.where(diag, 0, 1)
        lanes = slice(col * 128, (col + 1) * 128)
        mlanes = slice((col % 2) * 128, (col % 2 + 1) * 128)
        span = SEG // SUB_CHAINS
        p = [None] * SUB_CHAINS
        for step in range(span):
            for c in range(SUB_CHAINS):
                i = SEG - 1 - c * span - step
                r = slice(i * SUBLANES, (i + 1) * SUBLANES)
                nb = 1.0 / (1.0 + jnp.exp2(z_in[r, lanes]))
                nb = jnp.maximum(nb, mask_ref[mask_idx, r, mlanes])
                if step == 0:
                    raw_scr[r, lanes] = 1.0 - nb
                    p[c] = nb
                else:
                    pn = p[c] * nb
                    raw_scr[r, lanes] = p[c] - pn
                    p[c] = pn
        tot = _tree_product(p)
        excl, total = _suffix_products(tot)
        carry = jnp.where(diag, 1.0, carry_scr[stream, :, lanes])
        carry = jnp.where(tile[2] > 0, carry, 0.0)
        scale = [carry * excl]
        for c in range(1, SUB_CHAINS):
            scale.append(scale[c - 1] * p[c - 1])
        carry_scr[stream, :, lanes] = carry * total
        for i in range(0, SEG, 2):
            pair = [raw_scr[(i + d) * SUBLANES:(i + d + 1) * SUBLANES, lanes]
                    * scale[(SEG - 1 - i - d) // span] for d in range(2)]
            a_out[i * SUBLANES:(i + 2) * SUBLANES, lanes] = jnp.concatenate(
                pair, axis=0).astype(_BF16)

    def block(prv, cur, nxt, z_cur, a_cur, z_nxt, a_prv):
        for kind, stream, idx in BLOCK_ORDER:
            if kind == "c":
                scan(cur[stream], z_cur[stream], a_cur[stream], idx, stream)
            elif kind == "s":
                scores(nxt[stream], z_nxt[stream], idx)
            elif kind == "v":
                apply_values(prv[stream], a_prv[stream], idx, stream)
            else:
                flush(prv[stream], stream)

    zx, zy, ax, ay = (zx0, zx1), (zy0, zy1), (ax0, ax1), (ay0, ay1)
    first = tuple((jnp.int32(s), jnp.int32(0), jnp.int32(1)) for s in range(2))
    acc_scr[...] = jnp.zeros_like(acc_scr)
    carry_scr[...] = jnp.ones_like(carry_scr)
    for a_ref in ay:
        a_ref[...] = jnp.zeros_like(a_ref)
    for s in range(2):
        for h in range(HEADS_PER_STEP):
            scores(first[s], zx[s], h)

    def body(state):
        it, prv, cur = state
        useful = [jnp.logical_and(jnp.logical_and(prv[s][0] == cur[s][0], cur[s][1] < cur[s][0]),
                                  it > 0) for s in range(2)]
        zeroed = lax.cond(
            jnp.logical_or(useful[0], useful[1]),
            lambda: tuple((jnp.max(carry_scr[s]) <= 0.0).astype(jnp.int32) for s in range(2)),
            lambda: (jnp.int32(0), jnp.int32(0)))
        nxt = tuple(advance(cur[s], jnp.logical_and(useful[s], zeroed[s] > 0), s)
                    for s in range(2))

        @pl.when(it % 2 == 0)
        def _():
            block(prv, cur, nxt, zx, ax, zy, ay)

        @pl.when(it % 2 == 1)
        def _():
            block(prv, cur, nxt, zy, ay, zx, ax)

        return it + 1, cur, nxt

    n_iter, pending, _ = lax.while_loop(
        lambda state: state[2][0][2] + state[2][1][2] > 0, body, (jnp.int32(0), first, first))

    def drain(a_last):
        for s in range(2):
            for h in range(HEADS_PER_STEP):
                apply_values(pending[s], a_last[s], h, s)
            flush(pending[s], s)

    @pl.when(n_iter % 2 == 1)
    def _():
        drain(ax)

    @pl.when(n_iter % 2 == 0)
    def _():
        drain(ay)


def _attention(q0, q1, k3, vt3, mask):
    batch, seq, _ = k3.shape
    n_pairs = SB_HEADS // HEADS_PER_STEP
    lanes = HEADS_PER_STEP * SB_HEAD_DIM
    wide = HEADS_PER_STEP * PERM_TILE
    n_qblk = seq // PERM_TILE
    assert n_qblk % 2 == 0 and n_qblk >= 4
    seq_spec = pl.BlockSpec((None, seq, lanes), lambda b, hp: (b, 0, hp))
    return pl.pallas_call(
        functools.partial(_attn_kernel, n_qblk=n_qblk),
        grid=(batch, n_pairs),
        in_specs=[
            seq_spec, seq_spec, seq_spec,
            pl.BlockSpec((None, lanes, seq), lambda b, hp: (b, hp, 0)),
            pl.BlockSpec((2, PERM_TILE, PERM_TILE), lambda b, hp: (0, 0, 0)),
        ],
        out_specs=seq_spec,
        out_shape=jax.ShapeDtypeStruct((batch, seq, D_MODEL), _BF16),
        scratch_shapes=[pltpu.VMEM((PERM_TILE, wide), _F32)] * 4 + [
            pltpu.VMEM((PERM_TILE, wide), _BF16)] * 4 + [
            pltpu.VMEM((PERM_TILE, wide), _F32),
            pltpu.VMEM((2, lanes, PERM_TILE), _F32),
            pltpu.VMEM((2, SUBLANES, wide), _F32),
        ],
        compiler_params=_cparams(("arbitrary", "arbitrary")),
        name="sb_attention",
    )(q0, q1, k3, vt3, mask)


def _merge_kernel(ya_ref, yb_ref, sga_ref, sgb_ref, x_ref, wa_ref, wb_ref, wo_ref,
                  g1_ref, sc2_ref, sh2_ref, ln2_ref, x1_ref, h2_ref):
    ta = jnp.dot(ya_ref[...], wa_ref[...], preferred_element_type=_F32)
    tb = jnp.dot(yb_ref[...], wb_ref[...], preferred_element_type=_F32)
    merged = sga_ref[...].astype(_F32) * ta + sgb_ref[...].astype(_F32) * tb
    t = jnp.dot(merged.astype(_BF16), wo_ref[...], preferred_element_type=_F32)
    x1 = x_ref[...] + g1_ref[...] * t
    x1_ref[...] = x1
    h2_ref[...] = _rms_modulate(x1, ln2_ref[...], sc2_ref[...], sh2_ref[...]).astype(_BF16)


def _merge(ya, yb, sga, sgb, x2, wa, wb, wo, g1, sc2, sh2, ln2, seq):
    rows = x2.shape[0]
    tiles_per_seq = seq // ROW_TILE
    row_spec = pl.BlockSpec((ROW_TILE, D_MODEL), lambda m: (m, 0))
    w_spec = pl.BlockSpec((D_MODEL, D_MODEL), lambda m: (0, 0))
    mod_spec = pl.BlockSpec((None, 1, D_MODEL), lambda m: (m // tiles_per_seq, 0, 0))
    return pl.pallas_call(
        _merge_kernel,
        grid=(rows // ROW_TILE,),
        in_specs=[row_spec, row_spec, row_spec, row_spec, row_spec, w_spec, w_spec, w_spec,
                  mod_spec, mod_spec, mod_spec, pl.BlockSpec((1, D_MODEL), lambda m: (0, 0))],
        out_specs=[row_spec, row_spec],
        out_shape=[jax.ShapeDtypeStruct((rows, D_MODEL), _F32),
                   jax.ShapeDtypeStruct((rows, D_MODEL), _BF16)],
        compiler_params=_cparams(("arbitrary",)),
        name="merge_outproj",
    )(ya, yb, sga, sgb, x2, wa, wb, wo, g1, sc2, sh2, ln2)


def _ffn_kernel(h_ref, x_ref, wg_ref, wu_ref, wd_ref, g2_ref, o_ref):
    h = h_ref[...]
    acc = None
    for c in range(FFN_HIDDEN // FFN_CHUNK):
        cols = slice(c * FFN_CHUNK, (c + 1) * FFN_CHUNK)
        g = jnp.dot(h, wg_ref[:, cols], preferred_element_type=_F32)
        u = jnp.dot(h, wu_ref[:, cols], preferred_element_type=_F32)
        a = (g * (1.0 / (1.0 + jnp.exp(-g))) * u).astype(_BF16)
        d = jnp.dot(a, wd_ref[cols, :], preferred_element_type=_F32)
        acc = d if acc is None else acc + d
    o_ref[...] = x_ref[...] + g2_ref[...] * acc


def _ffn(h2, x1, wg, wu, wd, g2, seq):
    rows = x1.shape[0]
    tiles_per_seq = seq // ROW_TILE
    row_spec = pl.BlockSpec((ROW_TILE, D_MODEL), lambda m: (m, 0))
    return pl.pallas_call(
        _ffn_kernel,
        grid=(rows // ROW_TILE,),
        in_specs=[
            row_spec, row_spec,
            pl.BlockSpec((D_MODEL, FFN_HIDDEN), lambda m: (0, 0)),
            pl.BlockSpec((D_MODEL, FFN_HIDDEN), lambda m: (0, 0)),
            pl.BlockSpec((FFN_HIDDEN, D_MODEL), lambda m: (0, 0)),
            pl.BlockSpec((None, 1, D_MODEL), lambda m: (m // tiles_per_seq, 0, 0)),
        ],
        out_specs=row_spec,
        out_shape=jax.ShapeDtypeStruct((rows, D_MODEL), _F32),
        compiler_params=_cparams(("arbitrary",)),
        name="swiglu_ffn",
    )(h2, x1, wg, wu, wd, g2)


def _permute_tokens(x):
    b, s, d = x.shape
    return x.reshape(b, s // PERM_TILE, SUBLANES, SEG, d).transpose(0, 1, 3, 2, 4).reshape(b, s, d)


def _unpermute_tokens(x):
    b, s, d = x.shape
    return x.reshape(b, s // PERM_TILE, SEG, SUBLANES, d).transpose(0, 1, 3, 2, 4).reshape(b, s, d)


def _causal_mask():
    pos = np.arange(PERM_TILE)
    tok = (pos % SUBLANES) * SEG + pos // SUBLANES
    masked = (tok[:, None] >= tok[None, :]).astype(np.float32)
    return jnp.asarray(np.stack([masked, np.zeros_like(masked)]))


def _head_mean_matrix():
    idx = np.arange(256) // SB_HEAD_DIM
    return jnp.asarray((idx[:, None] == idx[None, :]).astype(np.float32) / SB_HEAD_DIM, dtype=_BF16)


def kernel(x, c, ada_w, ada_b, ln1_g, w_in, q_norm_g, k_norm_g, conv_w, w_branch_a, w_branch_b,
           w_out, ln2_g, w_ffn_gate, w_ffn_up, w_ffn_down):
    batch, seq, d = x.shape
    depth = ada_w.shape[0]
    assert d == D_MODEL and seq % ROW_TILE == 0 and ROW_TILE % PERM_TILE == 0
    assert batch <= SUBLANES and w_in.shape[2] == IN_WIDTH

    c_pad = jnp.zeros((SUBLANES, d), _F32).at[:batch].set(c)
    mod = _modulation(c_pad, ada_w, ada_b.reshape(depth, 1, -1))[:, :batch]
    mask = _causal_mask()
    gmat = _head_mean_matrix()
    q_scale = math.log2(math.e) / math.sqrt(SB_HEAD_DIM)

    xp = _permute_tokens(x).reshape(batch * seq, d)
    for l in range(depth):
        sh1, sc1, g1, sh2, sc2, g2 = [
            mod[l, :, i * d:(i + 1) * d].reshape(batch, 1, d) for i in range(6)]
        gq = jnp.tile(q_norm_g[l] * q_scale, SB_HEADS).reshape(1, d)
        gk = jnp.tile(k_norm_g[l], SB_HEADS).reshape(1, d)
        q0, q1, k, vt, yb, sga, sgb = _inproj(
            xp, sc1, sh1, ln1_g[l].reshape(1, d), w_in[l].astype(_BF16), gq, gk, gmat,
            conv_w[l], batch, seq)
        ya = _attention(q0.reshape(batch, seq, d), q1.reshape(batch, seq, d),
                        k.reshape(batch, seq, d), vt, mask)
        x1, h2 = _merge(
            ya.reshape(batch * seq, d), yb, sga, sgb, xp,
            w_branch_a[l].astype(_BF16), w_branch_b[l].astype(_BF16), w_out[l].astype(_BF16),
            g1, sc2, sh2, ln2_g[l].reshape(1, d), seq)
        xp = _ffn(h2, x1, w_ffn_gate[l].astype(_BF16), w_ffn_up[l].astype(_BF16),
                  w_ffn_down[l].astype(_BF16), g2, seq)
    return _unpermute_tokens(xp.reshape(batch, seq, d))
```
